```python
import math
import jax
import jax.numpy as jnp
from jax import lax
import numpy as np

D_MODEL = 1024
BATCH = 32
SEQ = 256
DEPTH = 4
DEC_BATCH = 8
DEC_SEQ = 2048
PAST_LEN = 512

GRID_W = 64
N_HEADS = 16
HEAD_DIM = D_MODEL // N_HEADS
N_KV_A = 16
N_KV_GQA = 4
D_FF = 4 * D_MODEL
N_MIXERS = 3
Q_BLOCK = 128
WINDOW = 128
WIN_H = 8
WIN_W = 16
NA_QCOLS = 16
NA_KCOLS = 32
ROPE_BASE = 10000.0
EPS = 1e-6
NEG_INF = -1e30
ADA_CHUNKS = 6

kernel_name = 'hybrid_diffusion_prefix_step'


def rmsnorm(x, g):
    xf = x.astype(jnp.float32)
    y = xf * lax.rsqrt(jnp.mean(xf * xf, axis=-1, keepdims=True) + EPS)
    return (y * g.astype(jnp.float32)).astype(x.dtype)


def rope_1d(x, pos):
    half = x.shape[-1] // 2
    freqs = jnp.exp(-math.log(ROPE_BASE) * jnp.arange(half, dtype=jnp.float32) / half)
    ang = pos[:, None] * freqs[None, :]
    shape = (x.shape[1],) + (1,) * (x.ndim - 3) + (half,)
    cos = jnp.cos(ang).reshape(shape).astype(x.dtype)
    sin = jnp.sin(ang).reshape(shape).astype(x.dtype)
    x1, x2 = x[..., :half], x[..., half:]
    return jnp.concatenate([x1 * cos - x2 * sin, x2 * cos + x1 * sin], axis=-1)


def rope_2d(x):
    t = jnp.arange(x.shape[1])
    rows = (t // GRID_W).astype(jnp.float32)
    cols = (t % GRID_W).astype(jnp.float32)
    half = x.shape[-1] // 2
    return jnp.concatenate([rope_1d(x[..., :half], rows), rope_1d(x[..., half:], cols)], axis=-1)


def split_qkv(qkv, n_kv):
    b_, t_ = qkv.shape[:2]
    nq = N_HEADS * HEAD_DIM
    nk = n_kv * HEAD_DIM
    q = qkv[..., :nq].reshape(b_, t_, n_kv, N_HEADS // n_kv, HEAD_DIM)
    k = qkv[..., nq:nq + nk].reshape(b_, t_, n_kv, HEAD_DIM)
    v = qkv[..., nq + nk:].reshape(b_, t_, n_kv, HEAD_DIM)
    return q, k, v


def attend(q, k, v, mask=None, sink=None):
    s = jnp.einsum('bqhgd,bkhd->bhgqk', q, k).astype(jnp.float32) * (HEAD_DIM ** -0.5)
    if mask is not None:
        s = jnp.where(mask, s, NEG_INF)
    if sink is not None:
        col = jnp.broadcast_to(sink.astype(jnp.float32)[None, :, :, None, None], s.shape[:-1] + (1,))
        p = jax.nn.softmax(jnp.concatenate([s, col], axis=-1), axis=-1)[..., :-1]
    else:
        p = jax.nn.softmax(s, axis=-1)
    return jnp.einsum('bhgqk,bkhd->bqhgd', p.astype(v.dtype), v)


def dense_blocked(q, k, v, sink=None):
    b_, t_ = q.shape[:2]
    nb = t_ // Q_BLOCK
    qb = jnp.moveaxis(q.reshape((b_, nb, Q_BLOCK) + q.shape[2:]), 1, 0)
    ob = lax.map(lambda blk: attend(blk, k, v, sink=sink), qb)
    return jnp.moveaxis(ob, 0, 1).reshape(q.shape)


def windowed_sink_attention(q, k, v, k_ctx, v_ctx, sink):
    b_, t_ = q.shape[:2]
    nb = t_ // Q_BLOCK
    n_ctx = k_ctx.shape[1]
    pad = ((0, 0), (Q_BLOCK, Q_BLOCK), (0, 0), (0, 0))
    kp, vp = jnp.pad(k, pad), jnp.pad(v, pad)
    qb = jnp.moveaxis(q.reshape((b_, nb, Q_BLOCK) + q.shape[2:]), 1, 0)
    ctx_mask = jnp.ones((Q_BLOCK, n_ctx), dtype=bool)

    def blk(args):
        i, q_blk = args
        start = i * Q_BLOCK
        k_band = lax.dynamic_slice_in_dim(kp, start, 3 * Q_BLOCK, axis=1)
        v_band = lax.dynamic_slice_in_dim(vp, start, 3 * Q_BLOCK, axis=1)
        qpos = start + jnp.arange(Q_BLOCK)
        kpos = start - Q_BLOCK + jnp.arange(3 * Q_BLOCK)
        valid = ((kpos >= 0) & (kpos < t_))[None, :] & (jnp.abs(qpos[:, None] - kpos[None, :]) <= WINDOW)
        mask = jnp.concatenate([valid, ctx_mask], axis=1)
        return attend(q_blk, jnp.concatenate([k_band, k_ctx], axis=1),
                      jnp.concatenate([v_band, v_ctx], axis=1), mask=mask, sink=sink)

    ob = lax.map(blk, (jnp.arange(nb), qb))
    return jnp.moveaxis(ob, 0, 1).reshape(q.shape)


def neighbourhood_attention(q, k, v, k_ctx, v_ctx, rpb):
    b_, t_, h_, d_ = q.shape
    rows = t_ // GRID_W
    kh = min(WIN_H, rows)
    ncb = GRID_W // NA_QCOLS
    scale = HEAD_DIM ** -0.5
    qc = np.arange(GRID_W).reshape(ncb, NA_QCOLS)
    band0 = np.clip(np.arange(ncb) * NA_QCOLS - WIN_W // 2, 0, GRID_W - NA_KCOLS)
    kc = band0[:, None] + np.arange(NA_KCOLS)
    c0 = np.clip(qc - WIN_W // 2, 0, GRID_W - WIN_W)
    col_valid = (kc[:, None, :] >= c0[:, :, None]) & (kc[:, None, :] < c0[:, :, None] + WIN_W)
    col_idx = np.clip(kc[:, None, :] - qc[:, :, None] + WIN_W - 1, 0, 2 * WIN_W - 2)
    loc_mask = jnp.asarray(col_valid)[:, :, None, :]
    rpb_c = rpb[:, :, col_idx]
    kg = k.reshape(b_, rows, GRID_W, h_, d_)
    vg = v.reshape(b_, rows, GRID_W, h_, d_)
    qrows = jnp.moveaxis(q.reshape(b_, rows, GRID_W, h_, d_), 1, 0).reshape(rows, b_, ncb, NA_QCOLS, h_, d_)
    n_loc = kh * NA_KCOLS

    def row(args):
        r, q_r = args
        r0 = jnp.clip(r - kh // 2, 0, rows - kh)
        kb = jnp.take(lax.dynamic_slice_in_dim(kg, r0, kh, axis=1), kc, axis=2)
        vb = jnp.take(lax.dynamic_slice_in_dim(vg, r0, kh, axis=1), kc, axis=2)
        row_idx = r0 + jnp.arange(kh) - r + WIN_H - 1
        bias = jnp.transpose(jnp.take(rpb_c, row_idx, axis=1), (0, 2, 3, 1, 4))
        s_loc = jnp.einsum('bnqhd,binchd->bhnqic', q_r, kb).astype(jnp.float32) * scale + bias.astype(jnp.float32)
        s_loc = jnp.where(loc_mask, s_loc, NEG_INF).reshape(b_, h_, ncb, NA_QCOLS, n_loc)
        s_ctx = jnp.einsum('bnqhd,bkhd->bhnqk', q_r, k_ctx).astype(jnp.float32) * scale
        p = jax.nn.softmax(jnp.concatenate([s_loc, s_ctx], axis=-1), axis=-1).astype(v.dtype)
        p_loc = p[..., :n_loc].reshape(b_, h_, ncb, NA_QCOLS, kh, NA_KCOLS)
        o = (jnp.einsum('bhnqic,binchd->bnqhd', p_loc, vb)
             + jnp.einsum('bhnqk,bkhd->bnqhd', p[..., n_loc:], v_ctx))
        return o.reshape(b_, GRID_W, h_, d_)

    o = lax.map(row, (jnp.arange(rows), qrows))
    return jnp.moveaxis(o, 0, 1).reshape(b_, t_, h_, d_)


def sublayers(x, cond, l, mix, ada_w, ada_b, norm_mix_g, norm_mlp_g, w_o, mlp_w1, mlp_b1, mlp_w2, mlp_b2):
    mods = jnp.split(jax.nn.silu(cond) @ ada_w[l] + ada_b[l], ADA_CHUNKS, axis=-1)
    sh_a, sc_a, g_a, sh_f, sc_f, g_f = [m_[:, None, :] for m_ in mods]
    h = rmsnorm(x, norm_mix_g[l]) * (1 + sc_a) + sh_a
    o, kv = mix(h)
    x = x + g_a * (o.reshape(x.shape) @ w_o[l])
    h = rmsnorm(x, norm_mlp_g[l]) * (1 + sc_f) + sh_f
    x = x + g_f * (jnp.square(jax.nn.relu(h @ mlp_w1[l] + mlp_b1[l])) @ mlp_w2[l] + mlp_b2[l])
    return x, kv


def setup_inputs(seed: int = 0) -> dict:
    key = jax.random.key(seed)
    ks = iter(jax.random.split(key, 32))

    def nrm(shape, s):
        return s * jax.random.normal(next(ks), shape, jnp.float32)

    n_a, n_b, n_c = (len(range(m, DEPTH, N_MIXERS)) for m in range(N_MIXERS))
    qkv_a = (N_HEADS + 2 * N_KV_A) * HEAD_DIM
    qkv_g = (N_HEADS + 2 * N_KV_GQA) * HEAD_DIM
    fan = D_MODEL ** -0.5
    return {
        'x_prompt': nrm((BATCH, SEQ, D_MODEL), 1.0),
        'x_sample': nrm((DEC_BATCH, DEC_SEQ, D_MODEL), 1.0),
        'cache_k_a': nrm((DEC_BATCH, n_a, PAST_LEN, N_KV_A, HEAD_DIM), 1.0),
        'cache_v_a': nrm((DEC_BATCH, n_a, PAST_LEN, N_KV_A, HEAD_DIM), 1.0),
        'cache_k_b': nrm((DEC_BATCH, n_b, PAST_LEN, N_KV_GQA, HEAD_DIM), 1.0),
        'cache_v_b': nrm((DEC_BATCH, n_b, PAST_LEN, N_KV_GQA, HEAD_DIM), 1.0),
        'cache_k_c': nrm((DEC_BATCH, n_c, PAST_LEN, N_KV_GQA, HEAD_DIM), 1.0),
        'cache_v_c': nrm((DEC_BATCH, n_c, PAST_LEN, N_KV_GQA, HEAD_DIM), 1.0),
        'c': nrm((DEC_BATCH, D_MODEL), 1.0),
        'c_ctx': nrm((D_MODEL,), 1.0),
        'ada_w': nrm((DEPTH, D_MODEL, ADA_CHUNKS * D_MODEL), 0.5 * fan),
        'ada_b': nrm((DEPTH, ADA_CHUNKS * D_MODEL), 0.01),
        'norm_mix_g': 1.0 + nrm((DEPTH, D_MODEL), 0.01),
        'norm_mlp_g': 1.0 + nrm((DEPTH, D_MODEL), 0.01),
        'w_o': nrm((DEPTH, D_MODEL, D_MODEL), fan),
        'mlp_w1': nrm((DEPTH, D_MODEL, D_FF), fan),
        'mlp_b1': nrm((DEPTH, D_FF), 0.01),
        'mlp_w2': nrm((DEPTH, D_FF, D_MODEL), D_FF ** -0.5),
        'mlp_b2': nrm((DEPTH, D_MODEL), 0.01),
        'w_qkv_a': nrm((n_a, D_MODEL, qkv_a), fan),
        'rpb_a': nrm((n_a, N_HEADS, 2 * WIN_H - 1, 2 * WIN_W - 1), 0.1),
        'w_qkv_b': nrm((n_b, D_MODEL, qkv_g), fan),
        'sink_b': nrm((n_b, N_HEADS), 0.5),
        'w_qkv_c': nrm((n_c, D_MODEL, qkv_g), fan),
        'q_norm_c': 1.0 + nrm((n_c, HEAD_DIM), 0.01),
        'k_norm_c': 1.0 + nrm((n_c, HEAD_DIM), 0.01),
        'final_norm_g': 1.0 + nrm((D_MODEL,), 0.01),
    }


def reference(x_prompt, x_sample, cache_k_a, cache_v_a, cache_k_b, cache_v_b, cache_k_c, cache_v_c,
              c, c_ctx, ada_w, ada_b, norm_mix_g, norm_mlp_g, w_o, mlp_w1, mlp_b1, mlp_w2, mlp_b2,
              w_qkv_a, rpb_a, w_qkv_b, sink_b, w_qkv_c, q_norm_c, k_norm_c, final_norm_g):
    w_qkv = (w_qkv_a, w_qkv_b, w_qkv_c)
    n_kv = (N_KV_A, N_KV_GQA, N_KV_GQA)
    caches_k = (cache_k_a, cache_k_b, cache_k_c)
    caches_v = (cache_v_a, cache_v_b, cache_v_c)
    new_k = ([], [], [])
    new_v = ([], [], [])
    shared = (ada_w, ada_b, norm_mix_g, norm_mlp_g, w_o, mlp_w1, mlp_b1, mlp_w2, mlp_b2)
    xp, xs = x_prompt, x_sample
    cond_ctx = c_ctx[None, :]

    for l in range(DEPTH):
        m, j = l % N_MIXERS, l // N_MIXERS
        sink = sink_b[j].reshape(N_KV_GQA, N_HEADS // N_KV_GQA) if m == 1 else None

        def ctx_mix(h):
            q, k, v = split_qkv(h @ w_qkv[m][j], n_kv[m])
            if m == 2:
                q, k = rmsnorm(q, q_norm_c[j]), rmsnorm(k, k_norm_c[j])
            return dense_blocked(q, k, v, sink=sink), (k, v)

        xp, (k_new, v_new) = sublayers(xp, cond_ctx, l, ctx_mix, *shared)
        new_k[m].append(k_new)
        new_v[m].append(v_new)

        def lat_mix(h):
            q, k, v = split_qkv(h @ w_qkv[m][j], n_kv[m])
            k_ctx, v_ctx = caches_k[m][:, j], caches_v[m][:, j]
            if m == 0:
                o = neighbourhood_attention(q[:, :, :, 0], k, v, k_ctx, v_ctx, rpb_a[j])
            elif m == 1:
                o = windowed_sink_attention(rope_2d(q), rope_2d(k), v, k_ctx, v_ctx, sink)
            else:
                q = rope_2d(rmsnorm(q, q_norm_c[j]))
                k = rope_2d(rmsnorm(k, k_norm_c[j]))
                o = dense_blocked(q, jnp.concatenate([k, k_ctx], axis=1), jnp.concatenate([v, v_ctx], axis=1))
            return o, None

        xs, _ = sublayers(xs, c, l, lat_mix, *shared)

    y_prompt = rmsnorm(xp, final_norm_g)
    y_sample = rmsnorm(xs, final_norm_g)
    k_a = jnp.stack(new_k[0], axis=1)
    v_a = jnp.stack(new_v[0], axis=1)
    k_b = jnp.stack(new_k[1], axis=1)
    v_b = jnp.stack(new_v[1], axis=1)
    k_c = jnp.stack(new_k[2], axis=1)
    v_c = jnp.stack(new_v[2], axis=1)
    return (y_prompt, y_sample, k_a, v_a, k_b, v_b, k_c, v_c)
```

```python
import functools
import math

import numpy as np
import jax
import jax.numpy as jnp
from jax import lax
from jax.experimental import pallas as pl
from jax.experimental.pallas import tpu as pltpu

D_MODEL = 1024
N_HEADS = 16
HEAD_DIM = 64
N_KV_A = 16
N_KV_GQA = 4
GQA_GROUP = N_HEADS // N_KV_GQA
D_FF = 4 * D_MODEL
ADA_CHUNKS = 6
N_MIXERS = 3
GRID_W = 64
WINDOW = 128
WIN_H = 8
WIN_W = 16
ROPE_BASE = 10000.0
EPS = 1e-6
NEG_INF = -1e30
Q_SCALE = HEAD_DIM ** -0.5

HEAD_SHIFT = HEAD_DIM.bit_length() - 1
LANES = 128
COND_ROWS = 16
TOKEN_TILE = 512
MLP_CHUNK = 1024
NA_ROWS = 4
NA_BAND = NA_ROWS + WIN_H
Q_BLOCK = 128
VMEM_LIMIT = 56 * 1024 * 1024

BF16 = jnp.bfloat16
F32 = jnp.float32


def _dot(a, b):
    return jnp.dot(a, b, preferred_element_type=F32)


def _dot_nt(a, b):
    return lax.dot_general(a, b, (((1,), (1,)), ((), ())), preferred_element_type=F32)


def _rmsnorm_rows(x, g):
    return x * lax.rsqrt(jnp.mean(x * x, axis=-1, keepdims=True) + EPS) * g


def _lane_is_low(shape=(1, LANES)):
    return lax.broadcasted_iota(jnp.int32, shape, len(shape) - 1) < HEAD_DIM


def _const_spec(shape):
    return pl.BlockSpec(shape, lambda *_: (0,) * len(shape), pipeline_mode=pl.Buffered(1))


def _params(semantics):
    return pltpu.CompilerParams(dimension_semantics=semantics, vmem_limit_bytes=VMEM_LIMIT)


def _ada_kernel(c_ref, w_ref, b_ref, o_ref):
    c = c_ref[...]
    s = (c / (1.0 + jnp.exp(-c))).astype(BF16)
    o_ref[...] = _dot(s, w_ref[...].astype(BF16)) + b_ref[...]


def _ada_mods(cond, ada_w, ada_b):
    depth, _, n_out = ada_w.shape
    nb = n_out // D_MODEL
    return pl.pallas_call(
        _ada_kernel,
        grid=(depth, nb),
        in_specs=[
            pl.BlockSpec((COND_ROWS, D_MODEL), lambda l, n: (0, 0)),
            pl.BlockSpec((None, D_MODEL, D_MODEL), lambda l, n: (l, 0, n)),
            pl.BlockSpec((None, 1, D_MODEL), lambda l, n: (l, 0, n)),
        ],
        out_specs=pl.BlockSpec((None, COND_ROWS, D_MODEL), lambda l, n: (l, 0, n)),
        out_shape=jax.ShapeDtypeStruct((depth, COND_ROWS, n_out), F32),
        compiler_params=_params(("arbitrary", "arbitrary")),
        name="ada_mods",
    )(cond, ada_w, ada_b.reshape(depth, 1, n_out))


def _head_rmsnorm(blk, gain):
    sq = blk * blk
    hi = sq.astype(BF16)
    lo = (sq - hi.astype(F32)).astype(BF16)
    r = lax.shift_right_logical(lax.broadcasted_iota(jnp.int32, (LANES, LANES), 0), HEAD_SHIFT)
    c = lax.shift_right_logical(lax.broadcasted_iota(jnp.int32, (LANES, LANES), 1), HEAD_SHIFT)
    avg = jnp.where(r == c, 1.0 / HEAD_DIM, 0.0).astype(BF16)
    ms = _dot(hi, avg) + _dot(lo, avg)
    return blk * lax.rsqrt(ms + EPS) * gain


def _rope_block(blk, cos, sin_signed):
    lane = lax.broadcasted_iota(jnp.int32, (1, LANES), 1)
    first = (lane & 16) == 0
    partner = jnp.where(first, pltpu.roll(blk, LANES - 16, 1), pltpu.roll(blk, 16, 1))
    return blk * cos + partner * sin_signed


def _qkv_kernel(*refs, n_kv, qk_norm, rope, dup_kv):
    x_ref, mods_ref, g_ref, wq_ref, wk_ref, wv_ref = refs[:6]
    pos = 6
    if qk_norm:
        qn_ref, kn_ref = refs[pos:pos + 2]
        pos += 2
    if rope:
        cos_ref, sin_ref = refs[pos:pos + 2]
        pos += 2
    q_ref, k_ref, v_ref = refs[pos:pos + 3]

    x = x_ref[...]
    h = _rmsnorm_rows(x, g_ref[...]) * (1.0 + mods_ref[1:2, :]) + mods_ref[0:1, :]
    h = h.astype(BF16)
    q = _dot(h, wq_ref[...])
    k = _dot(h, wk_ref[...])
    v = _dot(h, wv_ref[...])
    low = _lane_is_low()

    def finish(blk, gain_ref):
        if qk_norm:
            blk = _head_rmsnorm(blk, gain_ref[...])
        if rope:
            blk = _rope_block(blk, cos_ref[...], sin_ref[...])
        return blk

    for cb in range(N_HEADS * HEAD_DIM // LANES):
        sl = slice(cb * LANES, (cb + 1) * LANES)
        blk = finish(q[:, sl], qn_ref if qk_norm else None)
        q_ref[:, sl] = (blk * Q_SCALE).astype(q_ref.dtype)

    for cb in range(n_kv * HEAD_DIM // LANES):
        sl = slice(cb * LANES, (cb + 1) * LANES)
        kb = finish(k[:, sl], kn_ref if qk_norm else None)
        vb = v[:, sl]
        if dup_kv:
            for arr, ref in ((kb, k_ref), (vb, v_ref)):
                rolled = pltpu.roll(arr, HEAD_DIM, 1)
                ref[:, (2 * cb) * LANES:(2 * cb + 1) * LANES] = jnp.where(low, arr, rolled).astype(ref.dtype)
                ref[:, (2 * cb + 1) * LANES:(2 * cb + 2) * LANES] = jnp.where(low, rolled, arr).astype(ref.dtype)
        else:
            k_ref[:, sl] = kb.astype(k_ref.dtype)
            v_ref[:, sl] = vb.astype(v_ref.dtype)


def _qkv_proj(x, mods, gain, wq, wk, wv, *, n_kv, kv_dtype, dup_kv, qn=None, kn=None, rope=None, seq=None):
    tokens = x.shape[0]
    tm = TOKEN_TILE
    steps = tokens // tm
    tiles_per_cond = steps // mods.shape[0]
    nk = n_kv * HEAD_DIM
    nk_out = 2 * nk if dup_kv else nk
    in_specs = [
        pl.BlockSpec((tm, D_MODEL), lambda i: (i, 0)),
        pl.BlockSpec((None, ADA_CHUNKS, D_MODEL), lambda i: (i // tiles_per_cond, 0, 0)),
        _const_spec((1, D_MODEL)),
        _const_spec((D_MODEL, D_MODEL)),
        _const_spec((D_MODEL, nk)),
        _const_spec((D_MODEL, nk)),
    ]
    args = [x, mods, gain, wq, wk, wv]
    if qn is not None:
        in_specs += [_const_spec((1, LANES)), _const_spec((1, LANES))]
        args += [qn, kn]
    if rope is not None:
        tiles_per_seq = seq // tm
        in_specs += [pl.BlockSpec((tm, LANES), lambda i: (i % tiles_per_seq, 0))] * 2
        args += list(rope)
    return pl.pallas_call(
        functools.partial(_qkv_kernel, n_kv=n_kv, qk_norm=qn is not None, rope=rope is not None, dup_kv=dup_kv),
        grid=(steps,),
        in_specs=in_specs,
        out_specs=[
            pl.BlockSpec((tm, D_MODEL), lambda i: (i, 0)),
            pl.BlockSpec((tm, nk_out), lambda i: (i, 0)),
            pl.BlockSpec((tm, nk_out), lambda i: (i, 0)),
        ],
        out_shape=[
            jax.ShapeDtypeStruct((tokens, D_MODEL), BF16),
            jax.ShapeDtypeStruct((tokens, nk_out), kv_dtype),
            jax.ShapeDtypeStruct((tokens, nk_out), kv_dtype),
        ],
        compiler_params=_params(("arbitrary",)),
        name="qkv_proj",
    )(*args)


def _mlp_kernel(*refs, final):
    x_ref, o_ref, mods_ref, wo_ref, g_ref, w1_ref, b1_ref, w2_ref, b2_ref = refs[:9]
    fg_ref = refs[9] if final else None
    out_ref = refs[-1]

    x1 = x_ref[...] + mods_ref[2:3, :] * _dot(o_ref[...], wo_ref[...])
    h = _rmsnorm_rows(x1, g_ref[...]) * (1.0 + mods_ref[4:5, :]) + mods_ref[3:4, :]
    h = h.astype(BF16)
    acc = jnp.zeros_like(x1)
    for c in range(D_FF // MLP_CHUNK):
        sl = slice(c * MLP_CHUNK, (c + 1) * MLP_CHUNK)
        t = jnp.maximum(_dot(h, w1_ref[:, sl]) + b1_ref[:, sl], 0.0)
        acc = acc + _dot((t * t).astype(BF16), w2_ref[sl, :])
    x2 = x1 + mods_ref[5:6, :] * (acc + b2_ref[...])
    if final:
        x2 = _rmsnorm_rows(x2, fg_ref[...])
    out_ref[...] = x2


def _mlp_block(x, o, mods, wo, gain, w1, b1, w2, b2, final_gain=None):
    tokens = x.shape[0]
    tm = TOKEN_TILE
    steps = tokens // tm
    tiles_per_cond = steps // mods.shape[0]
    in_specs = [
        pl.BlockSpec((tm, D_MODEL), lambda i: (i, 0)),
        pl.BlockSpec((tm, D_MODEL), lambda i: (i, 0)),
        pl.BlockSpec((None, ADA_CHUNKS, D_MODEL), lambda i: (i // tiles_per_cond, 0, 0)),
        _const_spec((D_MODEL, D_MODEL)),
        _const_spec((1, D_MODEL)),
        _const_spec((D_MODEL, D_FF)),
        _const_spec((1, D_FF)),
        _const_spec((D_FF, D_MODEL)),
        _const_spec((1, D_MODEL)),
    ]
    args = [x, o, mods, wo, gain, w1, b1, w2, b2]
    if final_gain is not None:
        in_specs.append(_const_spec((1, D_MODEL)))
        args.append(final_gain)
    return pl.pallas_call(
        functools.partial(_mlp_kernel, final=final_gain is not None),
        grid=(steps,),
        in_specs=in_specs,
        out_specs=pl.BlockSpec((tm, D_MODEL), lambda i: (i, 0)),
        out_shape=jax.ShapeDtypeStruct((tokens, D_MODEL), F32),
        compiler_params=_params(("arbitrary",)),
        name="mlp_block",
    )(*args)


def _softmax_pv(scores, values, extra_logit=None):
    m = functools.reduce(jnp.maximum, [jnp.max(s, axis=-1, keepdims=True) for s in scores])
    if extra_logit is not None:
        m = jnp.maximum(m, extra_logit)
    es = [jnp.exp(s - m) for s in scores]
    denom = functools.reduce(jnp.add, [jnp.sum(e, axis=-1, keepdims=True) for e in es])
    if extra_logit is not None:
        denom = denom + jnp.exp(extra_logit - m)
    pv = functools.reduce(jnp.add, [_dot(e.astype(BF16), v) for e, v in zip(es, values)])
    return pv / denom


def _dup_head(blk, keep):
    return jnp.where(keep, blk, pltpu.roll(blk, HEAD_DIM, 1)).astype(BF16)


def _ctx_attn_kernel(*refs, n_kv, has_sink):
    if has_sink:
        sink_ref, q_ref, k_ref, v_ref, o_ref = refs
    else:
        q_ref, k_ref, v_ref, o_ref = refs
    low = _lane_is_low()
    for p in range(N_HEADS // 2):
        sl = slice(p * LANES, (p + 1) * LANES)
        qb = q_ref[:, sl]
        if n_kv == N_HEADS:
            kb = k_ref[:, sl].astype(BF16)
            vb = v_ref[:, sl].astype(BF16)
        else:
            kvh = (2 * p) // GQA_GROUP
            ksl = slice((kvh // 2) * LANES, (kvh // 2 + 1) * LANES)
            keep = low if kvh % 2 == 0 else jnp.logical_not(low)
            kb = _dup_head(k_ref[:, ksl], keep)
            vb = _dup_head(v_ref[:, ksl], keep)
        outs = []
        for half in range(2):
            sel = low if half == 0 else jnp.logical_not(low)
            qm = jnp.where(sel, qb, jnp.zeros_like(qb))
            s = _dot_nt(qm, kb)
            extra = None
            if has_sink:
                extra = jnp.full((s.shape[0], 1), sink_ref[2 * p + half], F32)
            outs.append(_softmax_pv([s], [vb], extra))
        o_ref[:, sl] = jnp.where(low, outs[0], outs[1]).astype(o_ref.dtype)


def _ctx_attention(q, k, v, *, batch, seq, n_kv, sink=None):
    nk = n_kv * HEAD_DIM
    in_specs = [
        pl.BlockSpec((seq, D_MODEL), lambda b: (b, 0)),
        pl.BlockSpec((seq, nk), lambda b: (b, 0)),
        pl.BlockSpec((seq, nk), lambda b: (b, 0)),
    ]
    args = [q, k, v]
    if sink is not None:
        in_specs.insert(0, pl.BlockSpec(memory_space=pltpu.SMEM))
        args.insert(0, sink)
    return pl.pallas_call(
        functools.partial(_ctx_attn_kernel, n_kv=n_kv, has_sink=sink is not None),
        grid=(batch,),
        in_specs=in_specs,
        out_specs=pl.BlockSpec((seq, D_MODEL), lambda b: (b, 0)),
        out_shape=jax.ShapeDtypeStruct((batch * seq, D_MODEL), BF16),
        compiler_params=_params(("arbitrary",)),
        name="ctx_attention",
    )(*args)


def _na_attn_kernel(q_ref, k_ref, v_ref, kc_ref, vc_ref, tab_ref, o_ref, *, rows):
    low = _lane_is_low()
    kcb = kc_ref[...].astype(BF16)
    vcb = vc_ref[...].astype(BF16)
    n_groups = rows // NA_ROWS
    q_rows = NA_ROWS * GRID_W
    k_rows = NA_BAND * GRID_W

    def body(g, carry):
        rq = pl.multiple_of(g * q_rows, q_rows)
        band0 = jnp.clip(g * NA_ROWS - WIN_H // 2, 0, rows - NA_BAND)
        ks = pl.multiple_of(band0 * GRID_W, GRID_W)
        typ = jnp.where(g == 0, 0, jnp.where(g == n_groups - 1, 2, 1))
        qb = q_ref[pl.ds(rq, q_rows), :]
        kb = k_ref[pl.ds(ks, k_rows), :]
        vb = v_ref[pl.ds(ks, k_rows), :]
        outs = []
        for half in range(2):
            sel = low if half == 0 else jnp.logical_not(low)
            qm = jnp.where(sel, qb, jnp.zeros_like(qb))
            s_loc = _dot_nt(qm, kb) + tab_ref[half, typ]
            s_ctx = _dot_nt(qm, kcb)
            outs.append(_softmax_pv([s_loc, s_ctx], [vb, vcb]))
        o_ref[pl.ds(rq, q_rows), :] = jnp.where(low, outs[0], outs[1]).astype(o_ref.dtype)
        return carry

    lax.fori_loop(0, n_groups, body, 0)


def _na_bias_tables(rpb, rows):
    h = rpb.shape[0]
    kh = min(WIN_H, rows)
    n_groups = rows // NA_ROWS
    qc = np.arange(GRID_W)[:, None]
    kc = np.arange(GRID_W)[None, :]
    c0 = np.clip(qc - WIN_W // 2, 0, GRID_W - WIN_W)
    col_valid = (kc >= c0) & (kc < c0 + WIN_W)
    pad_l = GRID_W - WIN_W
    rpb_pad = jnp.pad(rpb, ((0, 0), (0, 0), (pad_l, 2 * GRID_W - pad_l - rpb.shape[2])))
    toep = jnp.stack([lax.slice_in_dim(rpb_pad, GRID_W - 1 - q, 2 * GRID_W - 1 - q, axis=2)
                      for q in range(GRID_W)], axis=2)
    toep = jnp.where(jnp.asarray(col_valid), toep, NEG_INF)
    blocks = jnp.concatenate([toep, jnp.full((h, 1, GRID_W, GRID_W), NEG_INF, F32)], axis=1)
    invalid = rpb.shape[1]
    sel = np.full((3, NA_ROWS, NA_BAND), invalid, np.int32)
    for t, g in enumerate((0, 1, n_groups - 1)):
        band0 = int(np.clip(g * NA_ROWS - kh // 2, 0, rows - NA_BAND))
        for a in range(NA_ROWS):
            r = g * NA_ROWS + a
            r0 = int(np.clip(r - kh // 2, 0, rows - kh))
            for i in range(NA_BAND):
                kr = band0 + i
                if r0 <= kr < r0 + kh:
                    sel[t, a, i] = kr - r + WIN_H - 1
    tab = blocks[:, sel]
    tab = jnp.transpose(tab, (0, 1, 2, 4, 3, 5))
    return tab.reshape(h, 3, NA_ROWS * GRID_W, NA_BAND * GRID_W)


def _na_attention(q, k, v, cache_k, cache_v, tables, *, batch, seq, layer_slot):
    rows = seq // GRID_W
    n_pairs = N_HEADS // 2
    q3, k3, v3 = (a.reshape(batch, seq, D_MODEL) for a in (q, k, v))
    past = cache_k.shape[2]
    ck = cache_k.reshape(batch, cache_k.shape[1], past, D_MODEL)
    cv = cache_v.reshape(batch, cache_v.shape[1], past, D_MODEL)
    tok_spec = pl.BlockSpec((None, seq, LANES), lambda p, b: (b, 0, p))
    ctx_spec = pl.BlockSpec((None, None, past, LANES), lambda p, b: (b, layer_slot, 0, p))
    out = pl.pallas_call(
        functools.partial(_na_attn_kernel, rows=rows),
        grid=(n_pairs, batch),
        in_specs=[tok_spec, tok_spec, tok_spec, ctx_spec, ctx_spec,
                  pl.BlockSpec((2,) + tables.shape[1:], lambda p, b: (p, 0, 0, 0))],
        out_specs=tok_spec,
        out_shape=jax.ShapeDtypeStruct((batch, seq, D_MODEL), BF16),
        compiler_params=_params(("arbitrary", "arbitrary")),
        name="na_attention",
    )(q3, k3, v3, ck, cv, tables)
    return out.reshape(batch * seq, D_MODEL)


def _gqa_attn_kernel(*refs, seq, windowed):
    if windowed:
        sink_ref, q_ref, k_ref, v_ref, kc_ref, vc_ref, o_ref = refs
    else:
        q_ref, k_ref, v_ref, kc_ref, vc_ref, o_ref = refs
    kvh = pl.program_id(1)
    lane_half = lax.shift_right_logical(lax.broadcasted_iota(jnp.int32, (1, LANES), 1), HEAD_SHIFT)
    keep = lane_half == (kvh & 1)
    low = lane_half == 0
    kcb = _dup_head(kc_ref[...], keep)
    vcb = _dup_head(vc_ref[...], keep)
    band = Q_BLOCK + 2 * WINDOW
    stacked = GQA_GROUP * Q_BLOCK

    if windowed:
        q_shift = Q_BLOCK.bit_length() - 1
        head_of_row = lax.shift_right_logical(lax.broadcasted_iota(jnp.int32, (stacked, 1), 0), q_shift)
        sink_col = jnp.zeros((stacked, 1), F32)
        for g in range(GQA_GROUP):
            sink_col = jnp.where(head_of_row == g, sink_ref[kvh * GQA_GROUP + g], sink_col)
        q_off = lax.broadcasted_iota(jnp.int32, (stacked, band), 0) & (Q_BLOCK - 1)
        k_off = lax.broadcasted_iota(jnp.int32, (stacked, band), 1)
        rel = q_off - k_off

    def body(i, carry):
        r0 = pl.multiple_of(i * Q_BLOCK, Q_BLOCK)
        qblk = q_ref[pl.ds(r0, Q_BLOCK), :]
        zero = jnp.zeros((Q_BLOCK, LANES), qblk.dtype)
        parts = []
        for g in range(GQA_GROUP):
            blk = qblk[:, (g // 2) * LANES:(g // 2 + 1) * LANES]
            parts.append(jnp.where(low if g % 2 == 0 else jnp.logical_not(low), blk, zero))
        qs = jnp.concatenate(parts, axis=0)
        if windowed:
            ks = pl.multiple_of(jnp.clip(r0 - WINDOW, 0, seq - band), Q_BLOCK)
            kb = k_ref[pl.ds(ks, band), :]
            vb = v_ref[pl.ds(ks, band), :]
            dist = rel + (r0 - ks)
            s_lat = jnp.where(jnp.abs(dist) <= WINDOW, _dot_nt(qs, kb), NEG_INF)
            o = _softmax_pv([s_lat, _dot_nt(qs, kcb)], [vb, vcb], sink_col)
        else:
            o = _softmax_pv([_dot_nt(qs, k_ref[...]), _dot_nt(qs, kcb)], [v_ref[...], vcb])
        for c in range(GQA_GROUP // 2):
            even = o[(2 * c) * Q_BLOCK:(2 * c + 1) * Q_BLOCK]
            odd = o[(2 * c + 1) * Q_BLOCK:(2 * c + 2) * Q_BLOCK]
            o_ref[pl.ds(r0, Q_BLOCK), c * LANES:(c + 1) * LANES] = jnp.where(low, even, odd).astype(o_ref.dtype)
        return carry

    lax.fori_loop(0, seq // Q_BLOCK, body, 0)


def _gqa_attention(q, kd, vd, cache_k, cache_v, *, batch, seq, layer_slot, sink=None):
    group_w = GQA_GROUP * HEAD_DIM
    q3 = q.reshape(batch, seq, D_MODEL)
    kd3 = kd.reshape(batch, seq, N_KV_GQA * LANES)
    vd3 = vd.reshape(batch, seq, N_KV_GQA * LANES)
    past = cache_k.shape[2]
    ck = cache_k.reshape(batch, cache_k.shape[1], past, N_KV_GQA * HEAD_DIM)
    cv = cache_v.reshape(batch, cache_v.shape[1], past, N_KV_GQA * HEAD_DIM)
    q_spec = pl.BlockSpec((None, seq, group_w), lambda b, h: (b, 0, h))
    kv_spec = pl.BlockSpec((None, seq, LANES), lambda b, h: (b, 0, h))
    ctx_spec = pl.BlockSpec((None, None, past, LANES), lambda b, h: (b, layer_slot, 0, h // 2))
    in_specs = [q_spec, kv_spec, kv_spec, ctx_spec, ctx_spec]
    args = [q3, kd3, vd3, ck, cv]
    if sink is not None:
        in_specs.insert(0, pl.BlockSpec(memory_space=pltpu.SMEM))
        args.insert(0, sink)
    out = pl.pallas_call(
        functools.partial(_gqa_attn_kernel, seq=seq, windowed=sink is not None),
        grid=(batch, N_KV_GQA),
        in_specs=in_specs,
        out_specs=q_spec,
        out_shape=jax.ShapeDtypeStruct((batch, seq, D_MODEL), BF16),
        compiler_params=_params(("arbitrary", "arbitrary")),
        name="gqa_attention",
    )(*args)
    return out.reshape(batch * seq, D_MODEL)


def _rope_tables(seq):
    t = jnp.arange(seq)
    rows = (t // GRID_W).astype(F32)
    cols = (t % GRID_W).astype(F32)
    quarter = HEAD_DIM // 4
    freqs = jnp.exp(-math.log(ROPE_BASE) * jnp.arange(quarter, dtype=F32) / quarter)
    ang_r = rows[:, None] * freqs[None, :]
    ang_c = cols[:, None] * freqs[None, :]
    cos = jnp.concatenate([jnp.cos(ang_r)] * 2 + [jnp.cos(ang_c)] * 2, axis=1)
    sin = jnp.concatenate([-jnp.sin(ang_r), jnp.sin(ang_r), -jnp.sin(ang_c), jnp.sin(ang_c)], axis=1)
    return jnp.tile(cos, (1, LANES // HEAD_DIM)), jnp.tile(sin, (1, LANES // HEAD_DIM))


def kernel(x_prompt, x_sample, cache_k_a, cache_v_a, cache_k_b, cache_v_b, cache_k_c, cache_v_c, c, c_ctx,
           ada_w, ada_b, norm_mix_g, norm_mlp_g, w_o, mlp_w1, mlp_b1, mlp_w2, mlp_b2, w_qkv_a, rpb_a,
           w_qkv_b, sink_b, w_qkv_c, q_norm_c, k_norm_c, final_norm_g):
    batch, seq, _ = x_prompt.shape
    dec_batch, dec_seq, _ = x_sample.shape
    depth = ada_w.shape[0]
    assert dec_batch + 1 <= COND_ROWS and dec_seq % GRID_W == 0 and (dec_seq // GRID_W) % NA_ROWS == 0

    cond = jnp.concatenate([c, c_ctx[None, :], jnp.zeros((COND_ROWS - dec_batch - 1, D_MODEL), F32)], axis=0)
    mods = _ada_mods(cond, ada_w, ada_b).reshape(depth, COND_ROWS, ADA_CHUNKS, D_MODEL)

    w_qkv = (w_qkv_a, w_qkv_b, w_qkv_c)
    n_kv = (N_KV_A, N_KV_GQA, N_KV_GQA)
    caches_k = (cache_k_a, cache_k_b, cache_k_c)
    caches_v = (cache_v_a, cache_v_b, cache_v_c)
    new_k = ([], [], [])
    new_v = ([], [], [])
    rope = _rope_tables(dec_seq)
    nq = N_HEADS * HEAD_DIM

    xp = x_prompt.reshape(batch * seq, D_MODEL)
    xs = x_sample.reshape(dec_batch * dec_seq, D_MODEL)

    for l in range(depth):
        m, j = l % N_MIXERS, l // N_MIXERS
        nk = n_kv[m] * HEAD_DIM
        w = w_qkv[m][j].astype(BF16)
        wq, wk, wv = w[:, :nq], w[:, nq:nq + nk], w[:, nq + nk:]
        wo = w_o[l].astype(BF16)
        w1 = mlp_w1[l].astype(BF16)
        w2 = mlp_w2[l].astype(BF16)
        b1 = mlp_b1[l][None, :]
        b2 = mlp_b2[l][None, :]
        g_mix = norm_mix_g[l][None, :]
        g_mlp = norm_mlp_g[l][None, :]
        mods_ctx = mods[l, dec_batch:dec_batch + 1]
        mods_lat = mods[l, :dec_batch]
        final = final_norm_g[None, :] if l == depth - 1 else None
        norm_args = {}
        if m == 2:
            norm_args = dict(qn=jnp.tile(q_norm_c[j], LANES // HEAD_DIM)[None, :],
                             kn=jnp.tile(k_norm_c[j], LANES // HEAD_DIM)[None, :])
        sink = sink_b[j] if m == 1 else None

        q, k, v = _qkv_proj(xp, mods_ctx, g_mix, wq, wk, wv, n_kv=n_kv[m], kv_dtype=F32, dup_kv=False, **norm_args)
        o = _ctx_attention(q, k, v, batch=batch, seq=seq, n_kv=n_kv[m], sink=sink)
        xp = _mlp_block(xp, o, mods_ctx, wo, g_mlp, w1, b1, w2, b2, final)
        new_k[m].append(k.reshape(batch, seq, n_kv[m], HEAD_DIM))
        new_v[m].append(v.reshape(batch, seq, n_kv[m], HEAD_DIM))

        if m == 0:
            q, k, v = _qkv_proj(xs, mods_lat, g_mix, wq, wk, wv, n_kv=n_kv[m], kv_dtype=BF16, dup_kv=False)
            tables = _na_bias_tables(rpb_a[j], dec_seq // GRID_W)
            o = _na_attention(q, k, v, caches_k[m], caches_v[m], tables, batch=dec_batch, seq=dec_seq, layer_slot=j)
        else:
            q, k, v = _qkv_proj(xs, mods_lat, g_mix, wq, wk, wv, n_kv=n_kv[m], kv_dtype=BF16, dup_kv=True,
                                rope=rope, seq=dec_seq, **norm_args)
            o = _gqa_attention(q, k, v, caches_k[m], caches_v[m], batch=dec_batch, seq=dec_seq, layer_slot=j, sink=sink)
        xs = _mlp_block(xs, o, mods_lat, wo, g_mlp, w1, b1, w2, b2, final)

    y_prompt = xp.reshape(batch, seq, D_MODEL)
    y_sample = xs.reshape(dec_batch, dec_seq, D_MODEL)
    outs = [y_prompt, y_sample]
    for m in range(N_MIXERS):
        outs.append(jnp.stack(new_k[m], axis=1))
        outs.append(jnp.stack(new_v[m], axis=1))
    return tuple(outs)
```

```python
import functools
import math

import numpy as np
import jax
import jax.numpy as jnp
from jax import lax
from jax.experimental import pallas as pl
from jax.experimental.pallas import tpu as pltpu

D_MODEL = 1024
N_HEADS = 16
HEAD_DIM = 64
N_KV_A = 16
N_KV_GQA = 4
GQA_GROUP = N_HEADS // N_KV_GQA
D_FF = 4 * D_MODEL
ADA_CHUNKS = 6
N_MIXERS = 3
GRID_W = 64
WINDOW = 128
WIN_H = 8
WIN_W = 16
ROPE_BASE = 10000.0
EPS = 1e-6
NEG_INF = -1e30
LOG2E = math.log2(math.e)
Q_SCALE = HEAD_DIM ** -0.5 * LOG2E

HEAD_SHIFT = HEAD_DIM.bit_length() - 1
LANES = 128
COND_ROWS = 16
TOKEN_TILE = 512
MLP_CHUNK = 1024
NA_ROWS = 4
NA_BAND = NA_ROWS + WIN_H
Q_BLOCK = 128
GQA_BAND = Q_BLOCK + 2 * WINDOW
VMEM_LIMIT = 56 * 1024 * 1024

BF16 = jnp.bfloat16
F32 = jnp.float32


def _dot(a, b):
    return jnp.dot(a, b, preferred_element_type=F32)


def _dot_nt(a, b):
    return lax.dot_general(a, b, (((1,), (1,)), ((), ())), preferred_element_type=F32)


def _rmsnorm_rows(x, g):
    return x * lax.rsqrt(jnp.mean(x * x, axis=-1, keepdims=True) + EPS) * g


def _lane_is_low(shape=(1, LANES)):
    return lax.broadcasted_iota(jnp.int32, shape, len(shape) - 1) < HEAD_DIM


def _const_spec(shape):
    return pl.BlockSpec(shape, lambda *_: (0,) * len(shape), pipeline_mode=pl.Buffered(1))


def _params(semantics):
    return pltpu.CompilerParams(dimension_semantics=semantics, vmem_limit_bytes=VMEM_LIMIT)


def _ada_kernel(c_ref, w_ref, b_ref, o_ref):
    c = c_ref[...]
    s = (c / (1.0 + jnp.exp(-c))).astype(BF16)
    o_ref[...] = _dot(s, w_ref[...].astype(BF16)) + b_ref[...]


def _ada_mods(cond, ada_w, ada_b):
    depth, _, n_out = ada_w.shape
    nb = n_out // D_MODEL
    return pl.pallas_call(
        _ada_kernel,
        grid=(depth, nb),
        in_specs=[
            pl.BlockSpec((COND_ROWS, D_MODEL), lambda l, n: (0, 0)),
            pl.BlockSpec((None, D_MODEL, D_MODEL), lambda l, n: (l, 0, n)),
            pl.BlockSpec((None, 1, D_MODEL), lambda l, n: (l, 0, n)),
        ],
        out_specs=pl.BlockSpec((None, COND_ROWS, D_MODEL), lambda l, n: (l, 0, n)),
        out_shape=jax.ShapeDtypeStruct((depth, COND_ROWS, n_out), F32),
        compiler_params=_params(("arbitrary", "arbitrary")),
        name="ada_mods",
    )(cond, ada_w, ada_b.reshape(depth, 1, n_out))


def _head_rmsnorm(blk, gain):
    sq = blk * blk
    hi = sq.astype(BF16)
    lo = (sq - hi.astype(F32)).astype(BF16)
    r = lax.shift_right_logical(lax.broadcasted_iota(jnp.int32, (LANES, LANES), 0), HEAD_SHIFT)
    c = lax.shift_right_logical(lax.broadcasted_iota(jnp.int32, (LANES, LANES), 1), HEAD_SHIFT)
    avg = jnp.where(r == c, 1.0 / HEAD_DIM, 0.0).astype(BF16)
    ms = _dot(hi, avg) + _dot(lo, avg)
    return blk * lax.rsqrt(ms + EPS) * gain


def _rope_block(blk, cos, sin_signed):
    lane = lax.broadcasted_iota(jnp.int32, (1, LANES), 1)
    first = (lane & 16) == 0
    partner = jnp.where(first, pltpu.roll(blk, LANES - 16, 1), pltpu.roll(blk, 16, 1))
    return blk * cos + partner * sin_signed


def _qkv_kernel(*refs, n_kv, qk_norm, rope, dup_kv):
    x_ref, mods_ref, g_ref, wq_ref, wk_ref, wv_ref = refs[:6]
    pos = 6
    if qk_norm:
        qn_ref, kn_ref = refs[pos:pos + 2]
        pos += 2
    if rope:
        cos_ref, sin_ref = refs[pos:pos + 2]
        pos += 2
    q_ref, k_ref, v_ref = refs[pos:pos + 3]

    x = x_ref[...]
    h = _rmsnorm_rows(x, g_ref[...]) * (1.0 + mods_ref[1:2, :]) + mods_ref[0:1, :]
    h = h.astype(BF16)
    q = _dot(h, wq_ref[...])
    k = _dot(h, wk_ref[...])
    v = _dot(h, wv_ref[...])
    low = _lane_is_low()

    def finish(blk, gain_ref):
        if qk_norm:
            blk = _head_rmsnorm(blk, gain_ref[...])
        if rope:
            blk = _rope_block(blk, cos_ref[...], sin_ref[...])
        return blk

    for cb in range(N_HEADS * HEAD_DIM // LANES):
        sl = slice(cb * LANES, (cb + 1) * LANES)
        blk = finish(q[:, sl], qn_ref if qk_norm else None)
        q_ref[:, sl] = (blk * Q_SCALE).astype(q_ref.dtype)

    for cb in range(n_kv * HEAD_DIM // LANES):
        sl = slice(cb * LANES, (cb + 1) * LANES)
        kb = finish(k[:, sl], kn_ref if qk_norm else None)
        vb = v[:, sl]
        if dup_kv:
            for arr, ref in ((kb, k_ref), (vb, v_ref)):
                rolled = pltpu.roll(arr, HEAD_DIM, 1)
                ref[:, (2 * cb) * LANES:(2 * cb + 1) * LANES] = jnp.where(low, arr, rolled).astype(ref.dtype)
                ref[:, (2 * cb + 1) * LANES:(2 * cb + 2) * LANES] = jnp.where(low, rolled, arr).astype(ref.dtype)
        else:
            k_ref[:, sl] = kb.astype(k_ref.dtype)
            v_ref[:, sl] = vb.astype(v_ref.dtype)


def _qkv_proj(x, mods, gain, wq, wk, wv, *, n_kv, kv_dtype, dup_kv, qn=None, kn=None, rope=None, seq=None):
    tokens = x.shape[0]
    tm = TOKEN_TILE
    steps = tokens // tm
    tiles_per_cond = steps // mods.shape[0]
    nk = n_kv * HEAD_DIM
    nk_out = 2 * nk if dup_kv else nk
    in_specs = [
        pl.BlockSpec((tm, D_MODEL), lambda i: (i, 0)),
        pl.BlockSpec((None, ADA_CHUNKS, D_MODEL), lambda i: (i // tiles_per_cond, 0, 0)),
        _const_spec((1, D_MODEL)),
        _const_spec((D_MODEL, D_MODEL)),
        _const_spec((D_MODEL, nk)),
        _const_spec((D_MODEL, nk)),
    ]
    args = [x, mods, gain, wq, wk, wv]
    if qn is not None:
        in_specs += [_const_spec((1, LANES)), _const_spec((1, LANES))]
        args += [qn, kn]
    if rope is not None:
        tiles_per_seq = seq // tm
        in_specs += [pl.BlockSpec((tm, LANES), lambda i: (i % tiles_per_seq, 0))] * 2
        args += list(rope)
    return pl.pallas_call(
        functools.partial(_qkv_kernel, n_kv=n_kv, qk_norm=qn is not None, rope=rope is not None, dup_kv=dup_kv),
        grid=(steps,),
        in_specs=in_specs,
        out_specs=[
            pl.BlockSpec((tm, D_MODEL), lambda i: (i, 0)),
            pl.BlockSpec((tm, nk_out), lambda i: (i, 0)),
            pl.BlockSpec((tm, nk_out), lambda i: (i, 0)),
        ],
        out_shape=[
            jax.ShapeDtypeStruct((tokens, D_MODEL), BF16),
            jax.ShapeDtypeStruct((tokens, nk_out), kv_dtype),
            jax.ShapeDtypeStruct((tokens, nk_out), kv_dtype),
        ],
        compiler_params=_params(("arbitrary",)),
        name="qkv_proj",
    )(*args)


def _mlp_kernel(*refs, final):
    x_ref, o_ref, mods_ref, wo_ref, g_ref, w1_ref, b1_ref, w2_ref, b2_ref = refs[:9]
    fg_ref = refs[9] if final else None
    out_ref = refs[-1]

    x1 = x_ref[...] + mods_ref[2:3, :] * _dot(o_ref[...], wo_ref[...])
    h = _rmsnorm_rows(x1, g_ref[...]) * (1.0 + mods_ref[4:5, :]) + mods_ref[3:4, :]
    h = h.astype(BF16)
    acc = jnp.zeros_like(x1)
    for c in range(D_FF // MLP_CHUNK):
        sl = slice(c * MLP_CHUNK, (c + 1) * MLP_CHUNK)
        t = jnp.maximum(_dot(h, w1_ref[:, sl]) + b1_ref[:, sl], 0.0)
        acc = acc + _dot((t * t).astype(BF16), w2_ref[sl, :])
    x2 = x1 + mods_ref[5:6, :] * (acc + b2_ref[...])
    if final:
        x2 = _rmsnorm_rows(x2, fg_ref[...])
    out_ref[...] = x2


def _mlp_block(x, o, mods, wo, gain, w1, b1, w2, b2, final_gain=None):
    tokens = x.shape[0]
    tm = TOKEN_TILE
    steps = tokens // tm
    tiles_per_cond = steps // mods.shape[0]
    in_specs = [
        pl.BlockSpec((tm, D_MODEL), lambda i: (i, 0)),
        pl.BlockSpec((tm, D_MODEL), lambda i: (i, 0)),
        pl.BlockSpec((None, ADA_CHUNKS, D_MODEL), lambda i: (i // tiles_per_cond, 0, 0)),
        _const_spec((D_MODEL, D_MODEL)),
        _const_spec((1, D_MODEL)),
        _const_spec((D_MODEL, D_FF)),
        _const_spec((1, D_FF)),
        _const_spec((D_FF, D_MODEL)),
        _const_spec((1, D_MODEL)),
    ]
    args = [x, o, mods, wo, gain, w1, b1, w2, b2]
    if final_gain is not None:
        in_specs.append(_const_spec((1, D_MODEL)))
        args.append(final_gain)
    return pl.pallas_call(
        functools.partial(_mlp_kernel, final=final_gain is not None),
        grid=(steps,),
        in_specs=in_specs,
        out_specs=pl.BlockSpec((tm, D_MODEL), lambda i: (i, 0)),
        out_shape=jax.ShapeDtypeStruct((tokens, D_MODEL), F32),
        compiler_params=_params(("arbitrary",)),
        name="mlp_block",
    )(*args)


def _softmax_numerators(scores, extra_logit=None):
    m = functools.reduce(jnp.maximum, [jnp.max(s, axis=-1, keepdims=True) for s in scores])
    if extra_logit is not None:
        m = jnp.maximum(m, extra_logit)
    ps = [jnp.exp2(s - m).astype(BF16) for s in scores]
    return ps, (None if extra_logit is None else jnp.exp2(extra_logit - m))


def _with_ones(vals, value_lanes):
    return jnp.where(value_lanes, vals, jnp.ones_like(vals))


def _dup_head(blk, keep):
    return jnp.where(keep, blk, pltpu.roll(blk, HEAD_DIM, 1))


def _stack_heads(qb, low):
    zero = jnp.zeros_like(qb)
    return jnp.concatenate([jnp.where(low, qb, zero), jnp.where(jnp.logical_not(low), qb, zero)], axis=0)


def _values_with_ones(vb, low):
    return jnp.concatenate([_with_ones(vb, low), _with_ones(vb, jnp.logical_not(low))], axis=1)


def _stacked_pv(ps, v4s, low, extra=None):
    high = jnp.logical_not(low)
    acc = functools.reduce(jnp.add, [_dot(p, v) for p, v in zip(ps, v4s)])
    half = acc.shape[0] // 2
    outs = []
    for rows, lanes, ones_lanes in ((slice(0, half), slice(0, LANES), high), (slice(half, None), slice(LANES, None), low)):
        part = acc[rows, lanes]
        if extra is not None:
            part = part + jnp.where(ones_lanes, extra[rows], 0.0)
        denom = jnp.max(jnp.where(ones_lanes, part, 0.0), axis=-1, keepdims=True)
        outs.append(part * (1.0 / denom))
    return outs


def _ctx_attn_kernel(*refs, n_kv, has_sink):
    if has_sink:
        sink_ref, q_ref, k_ref, v_ref, o_ref = refs
    else:
        q_ref, k_ref, v_ref, o_ref = refs
    low = _lane_is_low()
    for p in range(N_HEADS // 2):
        sl = slice(p * LANES, (p + 1) * LANES)
        if n_kv == N_HEADS:
            kb = k_ref[:, sl].astype(BF16)
            vb = v_ref[:, sl].astype(BF16)
        else:
            kvh = (2 * p) // GQA_GROUP
            ksl = slice((kvh // 2) * LANES, (kvh // 2 + 1) * LANES)
            keep = low if kvh % 2 == 0 else jnp.logical_not(low)
            kb = _dup_head(k_ref[:, ksl], keep).astype(BF16)
            vb = _dup_head(v_ref[:, ksl], keep).astype(BF16)
        qs = _stack_heads(q_ref[:, sl], low)
        sink_col = None
        if has_sink:
            first_head = lax.broadcasted_iota(jnp.int32, (qs.shape[0], 1), 0) < q_ref.shape[0]
            sink_col = jnp.where(first_head, sink_ref[2 * p], sink_ref[2 * p + 1]) * LOG2E
        ps, extra = _softmax_numerators([_dot_nt(qs, kb)], sink_col)
        even, odd = _stacked_pv(ps, [_values_with_ones(vb, low)], low, extra)
        o_ref[:, sl] = jnp.where(low, even, odd).astype(o_ref.dtype)


def _ctx_attention(q, k, v, *, batch, seq, n_kv, sink=None):
    nk = n_kv * HEAD_DIM
    in_specs = [
        pl.BlockSpec((seq, D_MODEL), lambda b: (b, 0)),
        pl.BlockSpec((seq, nk), lambda b: (b, 0)),
        pl.BlockSpec((seq, nk), lambda b: (b, 0)),
    ]
    args = [q, k, v]
    if sink is not None:
        in_specs.insert(0, pl.BlockSpec(memory_space=pltpu.SMEM))
        args.insert(0, sink)
    return pl.pallas_call(
        functools.partial(_ctx_attn_kernel, n_kv=n_kv, has_sink=sink is not None),
        grid=(batch,),
        in_specs=in_specs,
        out_specs=pl.BlockSpec((seq, D_MODEL), lambda b: (b, 0)),
        out_shape=jax.ShapeDtypeStruct((batch * seq, D_MODEL), BF16),
        compiler_params=_params(("arbitrary",)),
        name="ctx_attention",
    )(*args)


def _na_group_geometry(rows):
    kh = min(WIN_H, rows)
    n_groups = rows // NA_ROWS
    masked = 2 * WIN_H - 1
    band0 = [int(np.clip(g * NA_ROWS - kh // 2, 0, rows - NA_BAND)) for g in range(n_groups)]
    table_of = [0 if g == 0 else (2 if g == n_groups - 1 else 1) for g in range(n_groups)]
    sel = np.full((3, NA_ROWS, NA_BAND), masked, np.int64)
    for g in (0, 1, n_groups - 1):
        for a in range(NA_ROWS):
            r = g * NA_ROWS + a
            r0 = int(np.clip(r - kh // 2, 0, rows - kh))
            for i in range(NA_BAND):
                kr = band0[g] + i
                if r0 <= kr < r0 + kh:
                    sel[table_of[g], a, i] = kr - r + WIN_H - 1
    return band0, table_of, sel


def _na_attn_kernel(q_ref, k_ref, v_ref, kc_ref, vc_ref, blk_ref, o_ref, tab_scr, v4_scr, *, rows):
    low = _lane_is_low()
    band0, table_of, sel = _na_group_geometry(rows)
    q_rows = NA_ROWS * GRID_W
    k_rows = NA_BAND * GRID_W
    n_groups = rows // NA_ROWS

    @pl.when(pl.program_id(1) == 0)
    def _():
        for half in range(2):
            for t in range(3):
                for a in range(NA_ROWS):
                    strip = [blk_ref[half, int(sel[t, a, i])] for i in range(NA_BAND)]
                    r = half * q_rows + a * GRID_W
                    tab_scr[t, r:r + GRID_W, :] = jnp.concatenate(strip, axis=1)

    kcb = kc_ref[...].astype(BF16)
    vc4 = _values_with_ones(vc_ref[...].astype(BF16), low)
    v4_scr[...] = _values_with_ones(v_ref[...], low)

    def scores_of(g):
        rq = g * q_rows
        ks = band0[g] * GRID_W
        qs = _stack_heads(q_ref[rq:rq + q_rows, :], low)
        return [_dot_nt(qs, k_ref[ks:ks + k_rows, :]) + tab_scr[table_of[g]], _dot_nt(qs, kcb)]

    nxt = scores_of(0)
    for g in range(n_groups):
        cur, nxt = nxt, (scores_of(g + 1) if g + 1 < n_groups else None)
        ks = band0[g] * GRID_W
        ps, _ = _softmax_numerators(cur)
        lo, hi = _stacked_pv(ps, [v4_scr[ks:ks + k_rows, :], vc4], low)
        o_ref[g * q_rows:(g + 1) * q_rows, :] = jnp.where(low, lo, hi).astype(o_ref.dtype)


def _na_bias_blocks(rpb):
    h = rpb.shape[0]
    qc = np.arange(GRID_W)[:, None]
    kc = np.arange(GRID_W)[None, :]
    c0 = np.clip(qc - WIN_W // 2, 0, GRID_W - WIN_W)
    col_valid = (kc >= c0) & (kc < c0 + WIN_W)
    pad_l = GRID_W - WIN_W
    rpb_pad = jnp.pad(rpb, ((0, 0), (0, 0), (pad_l, 2 * GRID_W - pad_l - rpb.shape[2])))
    toep = jnp.stack([lax.slice_in_dim(rpb_pad, GRID_W - 1 - q, 2 * GRID_W - 1 - q, axis=2)
                      for q in range(GRID_W)], axis=2)
    toep = jnp.where(jnp.asarray(col_valid), toep * LOG2E, NEG_INF)
    return jnp.concatenate([toep, jnp.full((h, 1, GRID_W, GRID_W), NEG_INF, F32)], axis=1)


def _na_attention(q, k, v, cache_k, cache_v, blocks, *, batch, seq, layer_slot):
    rows = seq // GRID_W
    n_pairs = N_HEADS // 2
    q3, k3, v3 = (a.reshape(batch, seq, D_MODEL) for a in (q, k, v))
    past = cache_k.shape[2]
    ck = cache_k.reshape(batch, cache_k.shape[1], past, D_MODEL)
    cv = cache_v.reshape(batch, cache_v.shape[1], past, D_MODEL)
    tok_spec = pl.BlockSpec((None, seq, LANES), lambda p, b: (b, 0, p))
    ctx_spec = pl.BlockSpec((None, None, past, LANES), lambda p, b: (b, layer_slot, 0, p))
    out = pl.pallas_call(
        functools.partial(_na_attn_kernel, rows=rows),
        grid=(n_pairs, batch),
        in_specs=[tok_spec, tok_spec, tok_spec, ctx_spec, ctx_spec,
                  pl.BlockSpec((2,) + blocks.shape[1:], lambda p, b: (p, 0, 0, 0))],
        out_specs=tok_spec,
        out_shape=jax.ShapeDtypeStruct((batch, seq, D_MODEL), BF16),
        scratch_shapes=[pltpu.VMEM((3, 2 * NA_ROWS * GRID_W, NA_BAND * GRID_W), F32),
                        pltpu.VMEM((seq, 2 * LANES), BF16)],
        compiler_params=_params(("arbitrary", "arbitrary")),
        name="na_attention",
    )(q3, k3, v3, ck, cv, blocks)
    return out.reshape(batch * seq, D_MODEL)


def _gqa_attn_kernel(*refs, seq, windowed):
    if windowed:
        sink_ref, q_ref, k_ref, v_ref, kc_ref, vc_ref, o_ref, v4_scr, bias_scr = refs
    else:
        q_ref, k_ref, v_ref, kc_ref, vc_ref, o_ref, v4_scr = refs
    kvh = pl.program_id(1)
    lane_half = lax.shift_right_logical(lax.broadcasted_iota(jnp.int32, (1, LANES), 1), HEAD_SHIFT)
    keep = lane_half == (kvh & 1)
    low = lane_half == 0
    high = jnp.logical_not(low)
    band = GQA_BAND
    stacked = GQA_GROUP * Q_BLOCK
    n_blocks = seq // Q_BLOCK

    head_order = list(range(0, GQA_GROUP, 2)) + list(range(1, GQA_GROUP, 2))

    kcb = _dup_head(kc_ref[...], keep).astype(BF16)
    vc4 = _values_with_ones(_dup_head(vc_ref[...], keep).astype(BF16), low)
    v4_scr[...] = _values_with_ones(v_ref[...], low)

    if windowed:
        q_shift = Q_BLOCK.bit_length() - 1
        block_of_row = lax.shift_right_logical(lax.broadcasted_iota(jnp.int32, (stacked, 1), 0), q_shift)
        sink_col = jnp.zeros((stacked, 1), F32)
        for blk_i, g in enumerate(head_order):
            sink_col = jnp.where(block_of_row == blk_i, sink_ref[kvh * GQA_GROUP + g] * LOG2E, sink_col)

        @pl.when((pl.program_id(0) == 0) & (kvh == 0))
        def _():
            q_off = lax.broadcasted_iota(jnp.int32, (stacked, band), 0) & (Q_BLOCK - 1)
            k_off = lax.broadcasted_iota(jnp.int32, (stacked, band), 1)
            for t in range(bias_scr.shape[0]):
                dist = q_off - k_off + t * Q_BLOCK
                bias_scr[t] = jnp.where(jnp.abs(dist) <= WINDOW, 0.0, NEG_INF)

    def key_rows(i):
        if not windowed:
            return slice(None)
        ks = int(np.clip(i * Q_BLOCK - WINDOW, 0, seq - band))
        return slice(ks, ks + band)

    def scores_of(i):
        r0 = i * Q_BLOCK
        qblk = q_ref[r0:r0 + Q_BLOCK, :]
        zero = jnp.zeros((Q_BLOCK, LANES), qblk.dtype)
        parts = []
        for g in head_order:
            blk = qblk[:, (g // 2) * LANES:(g // 2 + 1) * LANES]
            parts.append(jnp.where(low if g % 2 == 0 else high, blk, zero))
        qs = jnp.concatenate(parts, axis=0)
        keys = key_rows(i)
        s_lat = _dot_nt(qs, k_ref[keys, :])
        if windowed:
            s_lat = s_lat + bias_scr[(r0 - keys.start) // Q_BLOCK]
        return [s_lat, _dot_nt(qs, kcb)]

    nxt = scores_of(0)
    for i in range(n_blocks):
        cur, nxt = nxt, (scores_of(i + 1) if i + 1 < n_blocks else None)
        ps, extra = _softmax_numerators(cur, sink_col if windowed else None)
        even, odd = _stacked_pv(ps, [v4_scr[key_rows(i), :], vc4], low, extra)
        r0 = i * Q_BLOCK
        for c in range(GQA_GROUP // 2):
            rows_c = slice(c * Q_BLOCK, (c + 1) * Q_BLOCK)
            o_ref[r0:r0 + Q_BLOCK, c * LANES:(c + 1) * LANES] = jnp.where(low, even[rows_c], odd[rows_c]).astype(o_ref.dtype)


def _gqa_attention(q, kd, vd, cache_k, cache_v, *, batch, seq, layer_slot, sink=None):
    group_w = GQA_GROUP * HEAD_DIM
    q3 = q.reshape(batch, seq, D_MODEL)
    kd3 = kd.reshape(batch, seq, N_KV_GQA * LANES)
    vd3 = vd.reshape(batch, seq, N_KV_GQA * LANES)
    past = cache_k.shape[2]
    ck = cache_k.reshape(batch, cache_k.shape[1], past, N_KV_GQA * HEAD_DIM)
    cv = cache_v.reshape(batch, cache_v.shape[1], past, N_KV_GQA * HEAD_DIM)
    q_spec = pl.BlockSpec((None, seq, group_w), lambda b, h: (b, 0, h))
    kv_spec = pl.BlockSpec((None, seq, LANES), lambda b, h: (b, 0, h))
    ctx_spec = pl.BlockSpec((None, None, past, LANES), lambda b, h: (b, layer_slot, 0, h // 2))
    in_specs = [q_spec, kv_spec, kv_spec, ctx_spec, ctx_spec]
    args = [q3, kd3, vd3, ck, cv]
    scratch = [pltpu.VMEM((seq, 2 * LANES), BF16)]
    if sink is not None:
        in_specs.insert(0, pl.BlockSpec(memory_space=pltpu.SMEM))
        args.insert(0, sink)
        n_offsets = (GQA_BAND - Q_BLOCK) // Q_BLOCK + 1
        scratch.append(pltpu.VMEM((n_offsets, GQA_GROUP * Q_BLOCK, GQA_BAND), F32))
    out = pl.pallas_call(
        functools.partial(_gqa_attn_kernel, seq=seq, windowed=sink is not None),
        grid=(batch, N_KV_GQA),
        in_specs=in_specs,
        out_specs=q_spec,
        out_shape=jax.ShapeDtypeStruct((batch, seq, D_MODEL), BF16),
        scratch_shapes=scratch,
        compiler_params=_params(("arbitrary", "arbitrary")),
        name="gqa_attention",
    )(*args)
    return out.reshape(batch * seq, D_MODEL)


def _rope_tables(seq):
    t = jnp.arange(seq)
    rows = (t // GRID_W).astype(F32)
    cols = (t % GRID_W).astype(F32)
    quarter = HEAD_DIM // 4
    freqs = jnp.exp(-math.log(ROPE_BASE) * jnp.arange(quarter, dtype=F32) / quarter)
    ang_r = rows[:, None] * freqs[None, :]
    ang_c = cols[:, None] * freqs[None, :]
    cos = jnp.concatenate([jnp.cos(ang_r)] * 2 + [jnp.cos(ang_c)] * 2, axis=1)
    sin = jnp.concatenate([-jnp.sin(ang_r), jnp.sin(ang_r), -jnp.sin(ang_c), jnp.sin(ang_c)], axis=1)
    return jnp.tile(cos, (1, LANES // HEAD_DIM)), jnp.tile(sin, (1, LANES // HEAD_DIM))


def kernel(x_prompt, x_sample, cache_k_a, cache_v_a, cache_k_b, cache_v_b, cache_k_c, cache_v_c, c, c_ctx,
           ada_w, ada_b, norm_mix_g, norm_mlp_g, w_o, mlp_w1, mlp_b1, mlp_w2, mlp_b2, w_qkv_a, rpb_a,
           w_qkv_b, sink_b, w_qkv_c, q_norm_c, k_norm_c, final_norm_g):
    batch, seq, _ = x_prompt.shape
    dec_batch, dec_seq, _ = x_sample.shape
    depth = ada_w.shape[0]
    assert dec_batch + 1 <= COND_ROWS and dec_seq % GRID_W == 0
    assert (dec_seq // GRID_W) % NA_ROWS == 0 and dec_seq // GRID_W >= 3 * NA_ROWS

    cond = jnp.concatenate([c, c_ctx[None, :], jnp.zeros((COND_ROWS - dec_batch - 1, D_MODEL), F32)], axis=0)
    mods = _ada_mods(cond, ada_w, ada_b).reshape(depth, COND_ROWS, ADA_CHUNKS, D_MODEL)

    w_qkv = (w_qkv_a, w_qkv_b, w_qkv_c)
    n_kv = (N_KV_A, N_KV_GQA, N_KV_GQA)
    caches_k = (cache_k_a, cache_k_b, cache_k_c)
    caches_v = (cache_v_a, cache_v_b, cache_v_c)
    new_k = ([], [], [])
    new_v = ([], [], [])
    rope = _rope_tables(dec_seq)
    nq = N_HEADS * HEAD_DIM

    xp = x_prompt.reshape(batch * seq, D_MODEL)
    xs = x_sample.reshape(dec_batch * dec_seq, D_MODEL)

    for l in range(depth):
        m, j = l % N_MIXERS, l // N_MIXERS
        nk = n_kv[m] * HEAD_DIM
        w = w_qkv[m][j].astype(BF16)
        wq, wk, wv = w[:, :nq], w[:, nq:nq + nk], w[:, nq + nk:]
        wo = w_o[l].astype(BF16)
        w1 = mlp_w1[l].astype(BF16)
        w2 = mlp_w2[l].astype(BF16)
        b1 = mlp_b1[l][None, :]
        b2 = mlp_b2[l][None, :]
        g_mix = norm_mix_g[l][None, :]
        g_mlp = norm_mlp_g[l][None, :]
        mods_ctx = mods[l, dec_batch:dec_batch + 1]
        mods_lat = mods[l, :dec_batch]
        final = final_norm_g[None, :] if l == depth - 1 else None
        norm_args = {}
        if m == 2:
            norm_args = dict(qn=jnp.tile(q_norm_c[j], LANES // HEAD_DIM)[None, :],
                             kn=jnp.tile(k_norm_c[j], LANES // HEAD_DIM)[None, :])
        sink = sink_b[j] if m == 1 else None

        q, k, v = _qkv_proj(xp, mods_ctx, g_mix, wq, wk, wv, n_kv=n_kv[m], kv_dtype=F32, dup_kv=False, **norm_args)
        o = _ctx_attention(q, k, v, batch=batch, seq=seq, n_kv=n_kv[m], sink=sink)
        xp = _mlp_block(xp, o, mods_ctx, wo, g_mlp, w1, b1, w2, b2, final)
        new_k[m].append(k.reshape(batch, seq, n_kv[m], HEAD_DIM))
        new_v[m].append(v.reshape(batch, seq, n_kv[m], HEAD_DIM))

        if m == 0:
            q, k, v = _qkv_proj(xs, mods_lat, g_mix, wq, wk, wv, n_kv=n_kv[m], kv_dtype=BF16, dup_kv=False)
            o = _na_attention(q, k, v, caches_k[m], caches_v[m], _na_bias_blocks(rpb_a[j]),
                              batch=dec_batch, seq=dec_seq, layer_slot=j)
        else:
            q, k, v = _qkv_proj(xs, mods_lat, g_mix, wq, wk, wv, n_kv=n_kv[m], kv_dtype=BF16, dup_kv=True,
                                rope=rope, seq=dec_seq, **norm_args)
            o = _gqa_attention(q, k, v, caches_k[m], caches_v[m], batch=dec_batch, seq=dec_seq, layer_slot=j, sink=sink)
        xs = _mlp_block(xs, o, mods_lat, wo, g_mlp, w1, b1, w2, b2, final)

    y_prompt = xp.reshape(batch, seq, D_MODEL)
    y_sample = xs.reshape(dec_batch, dec_seq, D_MODEL)
    outs = [y_prompt, y_sample]
    for m in range(N_MIXERS):
        outs.append(jnp.stack(new_k[m], axis=1))
        outs.append(jnp.stack(new_v[m], axis=1))
    return tuple(outs)
```

```python
import functools
import math

import numpy as np
import jax
import jax.numpy as jnp
from jax import lax
from jax.experimental import pallas as pl
from jax.experimental.pallas import tpu as pltpu

D_MODEL = 1024
N_HEADS = 16
HEAD_DIM = 64
N_KV_A = 16
N_KV_GQA = 4
GQA_GROUP = N_HEADS // N_KV_GQA
D_FF = 4 * D_MODEL
ADA_CHUNKS = 6
N_MIXERS = 3
GRID_W = 64
WINDOW = 128
WIN_H = 8
WIN_W = 16
ROPE_BASE = 10000.0
EPS = 1e-6
NEG_INF = -1e30
LOG2E = math.log2(math.e)
Q_SCALE = HEAD_DIM ** -0.5 * LOG2E

HEAD_SHIFT = HEAD_DIM.bit_length() - 1
LANES = 128
COND_ROWS = 16
TOKEN_TILE = 512
MLP_CHUNK = 1024
NA_ROWS = 4
NA_BAND = NA_ROWS + WIN_H
Q_BLOCK = 128
GQA_BAND = Q_BLOCK + 2 * WINDOW
VMEM_LIMIT = 56 * 1024 * 1024

BF16 = jnp.bfloat16
F32 = jnp.float32


def _dot(a, b):
    return jnp.dot(a, b, preferred_element_type=F32)


def _dot_nt(a, b):
    return lax.dot_general(a, b, (((1,), (1,)), ((), ())), preferred_element_type=F32)


def _rmsnorm_rows(x, g):
    return x * lax.rsqrt(jnp.mean(x * x, axis=-1, keepdims=True) + EPS) * g


def _lane_is_low(shape=(1, LANES)):
    return lax.broadcasted_iota(jnp.int32, shape, len(shape) - 1) < HEAD_DIM


def _const_spec(shape):
    return pl.BlockSpec(shape, lambda *_: (0,) * len(shape), pipeline_mode=pl.Buffered(1))


def _params(semantics):
    return pltpu.CompilerParams(dimension_semantics=semantics, vmem_limit_bytes=VMEM_LIMIT)


def _ada_kernel(c_ref, w_ref, b_ref, o_ref):
    c = c_ref[...]
    s = (c / (1.0 + jnp.exp(-c))).astype(BF16)
    o_ref[...] = _dot(s, w_ref[...].astype(BF16)) + b_ref[...]


def _ada_mods(cond, ada_w, ada_b):
    depth, _, n_out = ada_w.shape
    nb = n_out // D_MODEL
    return pl.pallas_call(
        _ada_kernel,
        grid=(depth, nb),
        in_specs=[
            pl.BlockSpec((COND_ROWS, D_MODEL), lambda l, n: (0, 0)),
            pl.BlockSpec((None, D_MODEL, D_MODEL), lambda l, n: (l, 0, n)),
            pl.BlockSpec((None, 1, D_MODEL), lambda l, n: (l, 0, n)),
        ],
        out_specs=pl.BlockSpec((None, COND_ROWS, D_MODEL), lambda l, n: (l, 0, n)),
        out_shape=jax.ShapeDtypeStruct((depth, COND_ROWS, n_out), F32),
        compiler_params=_params(("arbitrary", "arbitrary")),
        name="ada_mods",
    )(cond, ada_w, ada_b.reshape(depth, 1, n_out))


def _head_rmsnorm(blk, gain):
    sq = blk * blk
    hi = sq.astype(BF16)
    lo = (sq - hi.astype(F32)).astype(BF16)
    r = lax.shift_right_logical(lax.broadcasted_iota(jnp.int32, (LANES, LANES), 0), HEAD_SHIFT)
    c = lax.shift_right_logical(lax.broadcasted_iota(jnp.int32, (LANES, LANES), 1), HEAD_SHIFT)
    avg = jnp.where(r == c, 1.0 / HEAD_DIM, 0.0).astype(BF16)
    ms = _dot(hi, avg) + _dot(lo, avg)
    return blk * lax.rsqrt(ms + EPS) * gain


def _rope_block(blk, cos, sin_signed):
    lane = lax.broadcasted_iota(jnp.int32, (1, LANES), 1)
    first = (lane & 16) == 0
    partner = jnp.where(first, pltpu.roll(blk, LANES - 16, 1), pltpu.roll(blk, 16, 1))
    return blk * cos + partner * sin_signed


def _qkv_kernel(*refs, n_kv, qk_norm, rope, dup_kv):
    x_ref, mods_ref, g_ref, wq_ref, wk_ref, wv_ref = refs[:6]
    pos = 6
    if qk_norm:
        qn_ref, kn_ref = refs[pos:pos + 2]
        pos += 2
    if rope:
        cos_ref, sin_ref = refs[pos:pos + 2]
        pos += 2
    q_ref, k_ref, v_ref = refs[pos:pos + 3]

    x = x_ref[...]
    h = _rmsnorm_rows(x, g_ref[...]) * (1.0 + mods_ref[1:2, :]) + mods_ref[0:1, :]
    h = h.astype(BF16)
    q = _dot(h, wq_ref[...])
    k = _dot(h, wk_ref[...])
    v = _dot(h, wv_ref[...])
    low = _lane_is_low()

    def finish(blk, gain_ref):
        if qk_norm:
            blk = _head_rmsnorm(blk, gain_ref[...])
        if rope:
            blk = _rope_block(blk, cos_ref[...], sin_ref[...])
        return blk

    for cb in range(N_HEADS * HEAD_DIM // LANES):
        sl = slice(cb * LANES, (cb + 1) * LANES)
        blk = finish(q[:, sl], qn_ref if qk_norm else None)
        q_ref[:, sl] = (blk * Q_SCALE).astype(q_ref.dtype)

    for cb in range(n_kv * HEAD_DIM // LANES):
        sl = slice(cb * LANES, (cb + 1) * LANES)
        kb = finish(k[:, sl], kn_ref if qk_norm else None)
        vb = v[:, sl]
        if dup_kv:
            for arr, ref in ((kb, k_ref), (vb, v_ref)):
                rolled = pltpu.roll(arr, HEAD_DIM, 1)
                ref[:, (2 * cb) * LANES:(2 * cb + 1) * LANES] = jnp.where(low, arr, rolled).astype(ref.dtype)
                ref[:, (2 * cb + 1) * LANES:(2 * cb + 2) * LANES] = jnp.where(low, rolled, arr).astype(ref.dtype)
        else:
            k_ref[:, sl] = kb.astype(k_ref.dtype)
            v_ref[:, sl] = vb.astype(v_ref.dtype)


def _qkv_ctx_kernel(*refs, n_kv, qk_norm, seq):
    x_ref, mods_ref, g_ref, wq_ref, wkt_ref, wvt_ref = refs[:6]
    if qk_norm:
        qn_ref, kn_ref = refs[6:8]
    q_ref, kt_ref, vt_ref = refs[-3:]

    h = _rmsnorm_rows(x_ref[...], g_ref[...]) * (1.0 + mods_ref[1:2, :]) + mods_ref[0:1, :]
    h = h.astype(BF16)
    q = _dot(h, wq_ref[...])
    for cb in range(N_HEADS * HEAD_DIM // LANES):
        sl = slice(cb * LANES, (cb + 1) * LANES)
        blk = _head_rmsnorm(q[:, sl], qn_ref[...]) if qk_norm else q[:, sl]
        q_ref[:, sl] = (blk * Q_SCALE).astype(q_ref.dtype)

    kt = _dot_nt(wkt_ref[...], h)
    vt = _dot_nt(wvt_ref[...], h)
    if qk_norm:
        k3 = kt.reshape(n_kv, HEAD_DIM, kt.shape[1])
        k3 = k3 * lax.rsqrt(jnp.mean(k3 * k3, axis=1, keepdims=True) + EPS)
        kt = k3.reshape(kt.shape) * kn_ref[...]
    for b in range(kt.shape[1] // seq):
        kt_ref[b] = kt[:, b * seq:(b + 1) * seq]
        vt_ref[b] = vt[:, b * seq:(b + 1) * seq]


def _qkv_proj_ctx(x, mods, gain, wq, wkt, wvt, *, n_kv, seq, qn=None, kn_col=None):
    tokens = x.shape[0]
    tm = TOKEN_TILE
    steps = tokens // tm
    nk = n_kv * HEAD_DIM
    in_specs = [
        pl.BlockSpec((tm, D_MODEL), lambda i: (i, 0)),
        pl.BlockSpec((None, ADA_CHUNKS, D_MODEL), lambda i: (0, 0, 0)),
        _const_spec((1, D_MODEL)),
        _const_spec((D_MODEL, D_MODEL)),
        _const_spec((nk, D_MODEL)),
        _const_spec((nk, D_MODEL)),
    ]
    args = [x, mods, gain, wq, wkt, wvt]
    if qn is not None:
        in_specs += [_const_spec((1, LANES)), _const_spec((nk, 1))]
        args += [qn, kn_col]
    kv_spec = pl.BlockSpec((tm // seq, nk, seq), lambda i: (i, 0, 0))
    kv_shape = jax.ShapeDtypeStruct((tokens // seq, nk, seq), F32)
    return pl.pallas_call(
        functools.partial(_qkv_ctx_kernel, n_kv=n_kv, qk_norm=qn is not None, seq=seq),
        grid=(steps,),
        in_specs=in_specs,
        out_specs=[pl.BlockSpec((tm, D_MODEL), lambda i: (i, 0)), kv_spec, kv_spec],
        out_shape=[jax.ShapeDtypeStruct((tokens, D_MODEL), BF16), kv_shape, kv_shape],
        compiler_params=_params(("arbitrary",)),
        name="qkv_proj_ctx",
    )(*args)


def _qkv_proj(x, mods, gain, wq, wk, wv, *, n_kv, kv_dtype, dup_kv, qn=None, kn=None, rope=None, seq=None):
    tokens = x.shape[0]
    tm = TOKEN_TILE
    steps = tokens // tm
    tiles_per_cond = steps // mods.shape[0]
    nk = n_kv * HEAD_DIM
    nk_out = 2 * nk if dup_kv else nk
    in_specs = [
        pl.BlockSpec((tm, D_MODEL), lambda i: (i, 0)),
        pl.BlockSpec((None, ADA_CHUNKS, D_MODEL), lambda i: (i // tiles_per_cond, 0, 0)),
        _const_spec((1, D_MODEL)),
        _const_spec((D_MODEL, D_MODEL)),
        _const_spec((D_MODEL, nk)),
        _const_spec((D_MODEL, nk)),
    ]
    args = [x, mods, gain, wq, wk, wv]
    if qn is not None:
        in_specs += [_const_spec((1, LANES)), _const_spec((1, LANES))]
        args += [qn, kn]
    if rope is not None:
        tiles_per_seq = seq // tm
        in_specs += [pl.BlockSpec((tm, LANES), lambda i: (i % tiles_per_seq, 0))] * 2
        args += list(rope)
    return pl.pallas_call(
        functools.partial(_qkv_kernel, n_kv=n_kv, qk_norm=qn is not None, rope=rope is not None, dup_kv=dup_kv),
        grid=(steps,),
        in_specs=in_specs,
        out_specs=[
            pl.BlockSpec((tm, D_MODEL), lambda i: (i, 0)),
            pl.BlockSpec((tm, nk_out), lambda i: (i, 0)),
            pl.BlockSpec((tm, nk_out), lambda i: (i, 0)),
        ],
        out_shape=[
            jax.ShapeDtypeStruct((tokens, D_MODEL), BF16),
            jax.ShapeDtypeStruct((tokens, nk_out), kv_dtype),
            jax.ShapeDtypeStruct((tokens, nk_out), kv_dtype),
        ],
        compiler_params=_params(("arbitrary",)),
        name="qkv_proj",
    )(*args)


def _mlp_kernel(*refs, final):
    x_ref, o_ref, mods_ref, wo_ref, g_ref, w1_ref, b1_ref, w2_ref, b2_ref = refs[:9]
    fg_ref = refs[9] if final else None
    out_ref = refs[-1]

    x1 = x_ref[...] + mods_ref[2:3, :] * _dot(o_ref[...], wo_ref[...])
    h = _rmsnorm_rows(x1, g_ref[...]) * (1.0 + mods_ref[4:5, :]) + mods_ref[3:4, :]
    h = h.astype(BF16)
    acc = jnp.zeros_like(x1)
    for c in range(D_FF // MLP_CHUNK):
        sl = slice(c * MLP_CHUNK, (c + 1) * MLP_CHUNK)
        t = jnp.maximum(_dot(h, w1_ref[:, sl]) + b1_ref[:, sl], 0.0)
        acc = acc + _dot((t * t).astype(BF16), w2_ref[sl, :])
    x2 = x1 + mods_ref[5:6, :] * (acc + b2_ref[...])
    if final:
        x2 = _rmsnorm_rows(x2, fg_ref[...])
    out_ref[...] = x2


def _mlp_block(x, o, mods, wo, gain, w1, b1, w2, b2, final_gain=None):
    tokens = x.shape[0]
    tm = TOKEN_TILE
    steps = tokens // tm
    tiles_per_cond = steps // mods.shape[0]
    in_specs = [
        pl.BlockSpec((tm, D_MODEL), lambda i: (i, 0)),
        pl.BlockSpec((tm, D_MODEL), lambda i: (i, 0)),
        pl.BlockSpec((None, ADA_CHUNKS, D_MODEL), lambda i: (i // tiles_per_cond, 0, 0)),
        _const_spec((D_MODEL, D_MODEL)),
        _const_spec((1, D_MODEL)),
        _const_spec((D_MODEL, D_FF)),
        _const_spec((1, D_FF)),
        _const_spec((D_FF, D_MODEL)),
        _const_spec((1, D_MODEL)),
    ]
    args = [x, o, mods, wo, gain, w1, b1, w2, b2]
    if final_gain is not None:
        in_specs.append(_const_spec((1, D_MODEL)))
        args.append(final_gain)
    return pl.pallas_call(
        functools.partial(_mlp_kernel, final=final_gain is not None),
        grid=(steps,),
        in_specs=in_specs,
        out_specs=pl.BlockSpec((tm, D_MODEL), lambda i: (i, 0)),
        out_shape=jax.ShapeDtypeStruct((tokens, D_MODEL), F32),
        compiler_params=_params(("arbitrary",)),
        name="mlp_block",
    )(*args)


def _softmax_numerators(scores, extra_logit=None):
    m = functools.reduce(jnp.maximum, [jnp.max(s, axis=-1, keepdims=True) for s in scores])
    if extra_logit is not None:
        m = jnp.maximum(m, extra_logit)
    ps = [jnp.exp2(s - m).astype(BF16) for s in scores]
    return ps, (None if extra_logit is None else jnp.exp2(extra_logit - m))


def _with_ones(vals, value_lanes):
    return jnp.where(value_lanes, vals, jnp.ones_like(vals))


def _dup_head(blk, keep):
    return jnp.where(keep, blk, pltpu.roll(blk, HEAD_DIM, 1))


def _stack_heads(qb, low):
    zero = jnp.zeros_like(qb)
    return jnp.concatenate([jnp.where(low, qb, zero), jnp.where(jnp.logical_not(low), qb, zero)], axis=0)


def _values_with_ones(vb, low):
    return jnp.concatenate([_with_ones(vb, low), _with_ones(vb, jnp.logical_not(low))], axis=1)


def _stacked_pv(ps, v4s, low, extra=None, values_transposed=False):
    high = jnp.logical_not(low)
    mm = _dot_nt if values_transposed else _dot
    acc = functools.reduce(jnp.add, [mm(p, v) for p, v in zip(ps, v4s)])
    half = acc.shape[0] // 2
    outs = []
    for rows, lanes, ones_lanes in ((slice(0, half), slice(0, LANES), high), (slice(half, None), slice(LANES, None), low)):
        part = acc[rows, lanes]
        if extra is not None:
            part = part + jnp.where(ones_lanes, extra[rows], 0.0)
        denom = jnp.max(jnp.where(ones_lanes, part, 0.0), axis=-1, keepdims=True)
        outs.append(part * (1.0 / denom))
    return outs


def _ctx_attn_kernel(*refs, n_kv, has_sink):
    if has_sink:
        sink_ref, q_ref, kt_ref, vt_ref, o_ref = refs
    else:
        q_ref, kt_ref, vt_ref, o_ref = refs
    low = _lane_is_low()
    seq = q_ref.shape[0]
    n_pairs = N_HEADS // 2
    ones = jnp.ones((HEAD_DIM, seq), BF16)

    def head_rows(h):
        kvh = h * n_kv // N_HEADS
        return slice(kvh * HEAD_DIM, (kvh + 1) * HEAD_DIM)

    def scores_of(p):
        kt = jnp.concatenate([kt_ref[head_rows(2 * p), :], kt_ref[head_rows(2 * p + 1), :]], axis=0)
        return [_dot(_stack_heads(q_ref[:, p * LANES:(p + 1) * LANES], low), kt.astype(BF16))]

    nxt = scores_of(0)
    for p in range(n_pairs):
        cur, nxt = nxt, (scores_of(p + 1) if p + 1 < n_pairs else None)
        sink_col = None
        if has_sink:
            first_head = lax.broadcasted_iota(jnp.int32, (2 * seq, 1), 0) < seq
            sink_col = jnp.where(first_head, sink_ref[2 * p], sink_ref[2 * p + 1]) * LOG2E
        ps, extra = _softmax_numerators(cur, sink_col)
        v4t = jnp.concatenate([vt_ref[head_rows(2 * p), :].astype(BF16), ones, ones,
                               vt_ref[head_rows(2 * p + 1), :].astype(BF16)], axis=0)
        even, odd = _stacked_pv(ps, [v4t], low, extra, values_transposed=True)
        o_ref[:, p * LANES:(p + 1) * LANES] = jnp.where(low, even, odd).astype(o_ref.dtype)


def _ctx_attention(q, kt, vt, *, batch, seq, n_kv, sink=None):
    nk = n_kv * HEAD_DIM
    in_specs = [
        pl.BlockSpec((seq, D_MODEL), lambda b: (b, 0)),
        pl.BlockSpec((None, nk, seq), lambda b: (b, 0, 0)),
        pl.BlockSpec((None, nk, seq), lambda b: (b, 0, 0)),
    ]
    args = [q, kt, vt]
    if sink is not None:
        in_specs.insert(0, pl.BlockSpec(memory_space=pltpu.SMEM))
        args.insert(0, sink)
    return pl.pallas_call(
        functools.partial(_ctx_attn_kernel, n_kv=n_kv, has_sink=sink is not None),
        grid=(batch,),
        in_specs=in_specs,
        out_specs=pl.BlockSpec((seq, D_MODEL), lambda b: (b, 0)),
        out_shape=jax.ShapeDtypeStruct((batch * seq, D_MODEL), BF16),
        compiler_params=_params(("arbitrary",)),
        name="ctx_attention",
    )(*args)


def _na_group_geometry(rows):
    kh = min(WIN_H, rows)
    n_groups = rows // NA_ROWS
    masked = 2 * WIN_H - 1
    band0 = [int(np.clip(g * NA_ROWS - kh // 2, 0, rows - NA_BAND)) for g in range(n_groups)]
    table_of = [0 if g == 0 else (2 if g == n_groups - 1 else 1) for g in range(n_groups)]
    sel = np.full((3, NA_ROWS, NA_BAND), masked, np.int64)
    for g in (0, 1, n_groups - 1):
        for a in range(NA_ROWS):
            r = g * NA_ROWS + a
            r0 = int(np.clip(r - kh // 2, 0, rows - kh))
            for i in range(NA_BAND):
                kr = band0[g] + i
                if r0 <= kr < r0 + kh:
                    sel[table_of[g], a, i] = kr - r + WIN_H - 1
    return band0, table_of, sel


def _na_attn_kernel(q_ref, k_ref, v_ref, kc_ref, vc_ref, blk_ref, o_ref, tab_scr, v4_scr, *, rows):
    low = _lane_is_low()
    band0, table_of, sel = _na_group_geometry(rows)
    q_rows = NA_ROWS * GRID_W
    k_rows = NA_BAND * GRID_W
    n_groups = rows // NA_ROWS

    @pl.when(pl.program_id(1) == 0)
    def _():
        for half in range(2):
            for t in range(3):
                for a in range(NA_ROWS):
                    strip = [blk_ref[half, int(sel[t, a, i])] for i in range(NA_BAND)]
                    r = half * q_rows + a * GRID_W
                    tab_scr[t, r:r + GRID_W, :] = jnp.concatenate(strip, axis=1)

    kcb = kc_ref[...].astype(BF16)
    vc4 = _values_with_ones(vc_ref[...].astype(BF16), low)
    v4_scr[...] = _values_with_ones(v_ref[...], low)

    def scores_of(g):
        rq = g * q_rows
        ks = band0[g] * GRID_W
        qs = _stack_heads(q_ref[rq:rq + q_rows, :], low)
        return [_dot_nt(qs, k_ref[ks:ks + k_rows, :]) + tab_scr[table_of[g]], _dot_nt(qs, kcb)]

    nxt = scores_of(0)
    for g in range(n_groups):
        cur, nxt = nxt, (scores_of(g + 1) if g + 1 < n_groups else None)
        ks = band0[g] * GRID_W
        ps, _ = _softmax_numerators(cur)
        lo, hi = _stacked_pv(ps, [v4_scr[ks:ks + k_rows, :], vc4], low)
        o_ref[g * q_rows:(g + 1) * q_rows, :] = jnp.where(low, lo, hi).astype(o_ref.dtype)


def _na_bias_blocks(rpb):
    h = rpb.shape[0]
    qc = np.arange(GRID_W)[:, None]
    kc = np.arange(GRID_W)[None, :]
    c0 = np.clip(qc - WIN_W // 2, 0, GRID_W - WIN_W)
    col_valid = (kc >= c0) & (kc < c0 + WIN_W)
    n_ri = rpb.shape[1]
    width = 2 * GRID_W
    pad_l = GRID_W - WIN_W
    rpb_pad = jnp.pad(rpb, ((0, 0), (0, 0), (pad_l, width - pad_l - rpb.shape[2])))
    flat = jnp.broadcast_to(rpb_pad[:, :, None, :], (h, n_ri, GRID_W, width)).reshape(h, n_ri, GRID_W * width)
    skew = flat[:, :, GRID_W - 1:GRID_W - 1 + GRID_W * (width - 1)].reshape(h, n_ri, GRID_W, width - 1)
    toep = jnp.where(jnp.asarray(col_valid), skew[..., :GRID_W] * LOG2E, NEG_INF)
    return jnp.concatenate([toep, jnp.full((h, 1, GRID_W, GRID_W), NEG_INF, F32)], axis=1)


def _na_attention(q, k, v, cache_k, cache_v, blocks, *, batch, seq, layer_slot):
    rows = seq // GRID_W
    n_pairs = N_HEADS // 2
    q3, k3, v3 = (a.reshape(batch, seq, D_MODEL) for a in (q, k, v))
    past = cache_k.shape[2]
    ck = cache_k.reshape(batch, cache_k.shape[1], past, D_MODEL)
    cv = cache_v.reshape(batch, cache_v.shape[1], past, D_MODEL)
    tok_spec = pl.BlockSpec((None, seq, LANES), lambda p, b: (b, 0, p))
    ctx_spec = pl.BlockSpec((None, None, past, LANES), lambda p, b: (b, layer_slot, 0, p))
    out = pl.pallas_call(
        functools.partial(_na_attn_kernel, rows=rows),
        grid=(n_pairs, batch),
        in_specs=[tok_spec, tok_spec, tok_spec, ctx_spec, ctx_spec,
                  pl.BlockSpec((2,) + blocks.shape[1:], lambda p, b: (p, 0, 0, 0))],
        out_specs=tok_spec,
        out_shape=jax.ShapeDtypeStruct((batch, seq, D_MODEL), BF16),
        scratch_shapes=[pltpu.VMEM((3, 2 * NA_ROWS * GRID_W, NA_BAND * GRID_W), F32),
                        pltpu.VMEM((seq, 2 * LANES), BF16)],
        compiler_params=_params(("arbitrary", "arbitrary")),
        name="na_attention",
    )(q3, k3, v3, ck, cv, blocks)
    return out.reshape(batch * seq, D_MODEL)


def _gqa_attn_kernel(*refs, seq, windowed):
    if windowed:
        sink_ref, q_ref, k_ref, v_ref, kc_ref, vc_ref, o_ref, v4_scr, bias_scr = refs
    else:
        q_ref, k_ref, v_ref, kc_ref, vc_ref, o_ref, v4_scr = refs
    kvh = pl.program_id(1)
    lane_half = lax.shift_right_logical(lax.broadcasted_iota(jnp.int32, (1, LANES), 1), HEAD_SHIFT)
    keep = lane_half == (kvh & 1)
    low = lane_half == 0
    high = jnp.logical_not(low)
    band = GQA_BAND
    stacked = GQA_GROUP * Q_BLOCK
    n_blocks = seq // Q_BLOCK

    head_order = list(range(0, GQA_GROUP, 2)) + list(range(1, GQA_GROUP, 2))

    kcb = _dup_head(kc_ref[...], keep).astype(BF16)
    vc4 = _values_with_ones(_dup_head(vc_ref[...], keep).astype(BF16), low)
    v4_scr[...] = _values_with_ones(v_ref[...], low)

    if windowed:
        q_shift = Q_BLOCK.bit_length() - 1
        block_of_row = lax.shift_right_logical(lax.broadcasted_iota(jnp.int32, (stacked, 1), 0), q_shift)
        sink_col = jnp.zeros((stacked, 1), F32)
        for blk_i, g in enumerate(head_order):
            sink_col = jnp.where(block_of_row == blk_i, sink_ref[kvh * GQA_GROUP + g] * LOG2E, sink_col)

        @pl.when((pl.program_id(0) == 0) & (kvh == 0))
        def _():
            q_off = lax.broadcasted_iota(jnp.int32, (stacked, band), 0) & (Q_BLOCK - 1)
            k_off = lax.broadcasted_iota(jnp.int32, (stacked, band), 1)
            for t in range(bias_scr.shape[0]):
                dist = q_off - k_off + t * Q_BLOCK
                bias_scr[t] = jnp.where(jnp.abs(dist) <= WINDOW, 0.0, NEG_INF)

    def key_rows(i):
        if not windowed:
            return slice(None)
        ks = int(np.clip(i * Q_BLOCK - WINDOW, 0, seq - band))
        return slice(ks, ks + band)

    def scores_of(i):
        r0 = i * Q_BLOCK
        qblk = q_ref[r0:r0 + Q_BLOCK, :]
        zero = jnp.zeros((Q_BLOCK, LANES), qblk.dtype)
        parts = []
        for g in head_order:
            blk = qblk[:, (g // 2) * LANES:(g // 2 + 1) * LANES]
            parts.append(jnp.where(low if g % 2 == 0 else high, blk, zero))
        qs = jnp.concatenate(parts, axis=0)
        keys = key_rows(i)
        s_lat = _dot_nt(qs, k_ref[keys, :])
        if windowed:
            s_lat = s_lat + bias_scr[(r0 - keys.start) // Q_BLOCK]
        return [s_lat, _dot_nt(qs, kcb)]

    nxt = scores_of(0)
    for i in range(n_blocks):
        cur, nxt = nxt, (scores_of(i + 1) if i + 1 < n_blocks else None)
        ps, extra = _softmax_numerators(cur, sink_col if windowed else None)
        even, odd = _stacked_pv(ps, [v4_scr[key_rows(i), :], vc4], low, extra)
        r0 = i * Q_BLOCK
        for c in range(GQA_GROUP // 2):
            rows_c = slice(c * Q_BLOCK, (c + 1) * Q_BLOCK)
            o_ref[r0:r0 + Q_BLOCK, c * LANES:(c + 1) * LANES] = jnp.where(low, even[rows_c], odd[rows_c]).astype(o_ref.dtype)


def _gqa_attention(q, kd, vd, cache_k, cache_v, *, batch, seq, layer_slot, sink=None):
    group_w = GQA_GROUP * HEAD_DIM
    q3 = q.reshape(batch, seq, D_MODEL)
    kd3 = kd.reshape(batch, seq, N_KV_GQA * LANES)
    vd3 = vd.reshape(batch, seq, N_KV_GQA * LANES)
    past = cache_k.shape[2]
    ck = cache_k.reshape(batch, cache_k.shape[1], past, N_KV_GQA * HEAD_DIM)
    cv = cache_v.reshape(batch, cache_v.shape[1], past, N_KV_GQA * HEAD_DIM)
    q_spec = pl.BlockSpec((None, seq, group_w), lambda b, h: (b, 0, h))
    kv_spec = pl.BlockSpec((None, seq, LANES), lambda b, h: (b, 0, h))
    ctx_spec = pl.BlockSpec((None, None, past, LANES), lambda b, h: (b, layer_slot, 0, h // 2))
    in_specs = [q_spec, kv_spec, kv_spec, ctx_spec, ctx_spec]
    args = [q3, kd3, vd3, ck, cv]
    scratch = [pltpu.VMEM((seq, 2 * LANES), BF16)]
    if sink is not None:
        in_specs.insert(0, pl.BlockSpec(memory_space=pltpu.SMEM))
        args.insert(0, sink)
        n_offsets = (GQA_BAND - Q_BLOCK) // Q_BLOCK + 1
        scratch.append(pltpu.VMEM((n_offsets, GQA_GROUP * Q_BLOCK, GQA_BAND), F32))
    out = pl.pallas_call(
        functools.partial(_gqa_attn_kernel, seq=seq, windowed=sink is not None),
        grid=(batch, N_KV_GQA),
        in_specs=in_specs,
        out_specs=q_spec,
        out_shape=jax.ShapeDtypeStruct((batch, seq, D_MODEL), BF16),
        scratch_shapes=scratch,
        compiler_params=_params(("arbitrary", "arbitrary")),
        name="gqa_attention",
    )(*args)
    return out.reshape(batch * seq, D_MODEL)


def _rope_tables(seq):
    t = jnp.arange(seq)
    rows = (t // GRID_W).astype(F32)
    cols = (t % GRID_W).astype(F32)
    quarter = HEAD_DIM // 4
    freqs = jnp.exp(-math.log(ROPE_BASE) * jnp.arange(quarter, dtype=F32) / quarter)
    ang_r = rows[:, None] * freqs[None, :]
    ang_c = cols[:, None] * freqs[None, :]
    cos = jnp.concatenate([jnp.cos(ang_r)] * 2 + [jnp.cos(ang_c)] * 2, axis=1)
    sin = jnp.concatenate([-jnp.sin(ang_r), jnp.sin(ang_r), -jnp.sin(ang_c), jnp.sin(ang_c)], axis=1)
    return jnp.tile(cos, (1, LANES // HEAD_DIM)), jnp.tile(sin, (1, LANES // HEAD_DIM))


def kernel(x_prompt, x_sample, cache_k_a, cache_v_a, cache_k_b, cache_v_b, cache_k_c, cache_v_c, c, c_ctx,
           ada_w, ada_b, norm_mix_g, norm_mlp_g, w_o, mlp_w1, mlp_b1, mlp_w2, mlp_b2, w_qkv_a, rpb_a,
           w_qkv_b, sink_b, w_qkv_c, q_norm_c, k_norm_c, final_norm_g):
    batch, seq, _ = x_prompt.shape
    dec_batch, dec_seq, _ = x_sample.shape
    depth = ada_w.shape[0]
    assert dec_batch + 1 <= COND_ROWS and dec_seq % GRID_W == 0
    assert (dec_seq // GRID_W) % NA_ROWS == 0 and dec_seq // GRID_W >= 3 * NA_ROWS

    cond = jnp.concatenate([c, c_ctx[None, :], jnp.zeros((COND_ROWS - dec_batch - 1, D_MODEL), F32)], axis=0)
    mods = _ada_mods(cond, ada_w, ada_b).reshape(depth, COND_ROWS, ADA_CHUNKS, D_MODEL)

    w_qkv = (w_qkv_a, w_qkv_b, w_qkv_c)
    n_kv = (N_KV_A, N_KV_GQA, N_KV_GQA)
    caches_k = (cache_k_a, cache_k_b, cache_k_c)
    caches_v = (cache_v_a, cache_v_b, cache_v_c)
    new_k = ([], [], [])
    new_v = ([], [], [])
    rope = _rope_tables(dec_seq)
    nq = N_HEADS * HEAD_DIM

    xp = x_prompt.reshape(batch * seq, D_MODEL)
    xs = x_sample.reshape(dec_batch * dec_seq, D_MODEL)

    for l in range(depth):
        m, j = l % N_MIXERS, l // N_MIXERS
        nk = n_kv[m] * HEAD_DIM
        w = w_qkv[m][j].astype(BF16)
        wq, wk, wv = w[:, :nq], w[:, nq:nq + nk], w[:, nq + nk:]
        wo = w_o[l].astype(BF16)
        w1 = mlp_w1[l].astype(BF16)
        w2 = mlp_w2[l].astype(BF16)
        b1 = mlp_b1[l][None, :]
        b2 = mlp_b2[l][None, :]
        g_mix = norm_mix_g[l][None, :]
        g_mlp = norm_mlp_g[l][None, :]
        mods_ctx = mods[l, dec_batch:dec_batch + 1]
        mods_lat = mods[l, :dec_batch]
        final = final_norm_g[None, :] if l == depth - 1 else None
        norm_args, ctx_norm_args = {}, {}
        if m == 2:
            qn = jnp.tile(q_norm_c[j], LANES // HEAD_DIM)[None, :]
            norm_args = dict(qn=qn, kn=jnp.tile(k_norm_c[j], LANES // HEAD_DIM)[None, :])
            ctx_norm_args = dict(qn=qn, kn_col=jnp.tile(k_norm_c[j], n_kv[m])[:, None])
        sink = sink_b[j] if m == 1 else None

        q, kt, vt = _qkv_proj_ctx(xp, mods_ctx, g_mix, wq, wk.T, wv.T, n_kv=n_kv[m], seq=seq, **ctx_norm_args)
        o = _ctx_attention(q, kt, vt, batch=batch, seq=seq, n_kv=n_kv[m], sink=sink)
        xp = _mlp_block(xp, o, mods_ctx, wo, g_mlp, w1, b1, w2, b2, final)
        new_k[m].append(kt)
        new_v[m].append(vt)

        if m == 0:
            q, k, v = _qkv_proj(xs, mods_lat, g_mix, wq, wk, wv, n_kv=n_kv[m], kv_dtype=BF16, dup_kv=False)
            o = _na_attention(q, k, v, caches_k[m], caches_v[m], _na_bias_blocks(rpb_a[j]),
                              batch=dec_batch, seq=dec_seq, layer_slot=j)
        else:
            q, k, v = _qkv_proj(xs, mods_lat, g_mix, wq, wk, wv, n_kv=n_kv[m], kv_dtype=BF16, dup_kv=True,
                                rope=rope, seq=dec_seq, **norm_args)
            o = _gqa_attention(q, k, v, caches_k[m], caches_v[m], batch=dec_batch, seq=dec_seq, layer_slot=j, sink=sink)
        xs = _mlp_block(xs, o, mods_lat, wo, g_mlp, w1, b1, w2, b2, final)

    y_prompt = xp.reshape(batch, seq, D_MODEL)
    y_sample = xs.reshape(dec_batch, dec_seq, D_MODEL)
    outs = [y_prompt, y_sample]
    for m in range(N_MIXERS):
        for per_layer in (new_k[m], new_v[m]):
            t = jnp.stack(per_layer, axis=1).reshape(batch, len(per_layer), n_kv[m], HEAD_DIM, seq)
            outs.append(jnp.transpose(t, (0, 1, 4, 2, 3)))
    return tuple(outs)
```

```python
import functools
import math

import numpy as np
import jax
import jax.numpy as jnp
from jax import lax
from jax.experimental import pallas as pl
from jax.experimental.pallas import tpu as pltpu

D_MODEL = 1024
N_HEADS = 16
HEAD_DIM = 64
N_KV_A = 16
N_KV_GQA = 4
GQA_GROUP = N_HEADS // N_KV_GQA
D_FF = 4 * D_MODEL
ADA_CHUNKS = 6
N_MIXERS = 3
GRID_W = 64
WINDOW = 128
WIN_H = 8
WIN_W = 16
ROPE_BASE = 10000.0
EPS = 1e-6
NEG_INF = -1e30
LOG2E = math.log2(math.e)
Q_SCALE = HEAD_DIM ** -0.5 * LOG2E

HEAD_SHIFT = HEAD_DIM.bit_length() - 1
LANES = 128
COND_ROWS = 16
TOKEN_TILE = 512
MLP_CHUNK = 1024
QKV_TILE = 1024
QKV_SUB_TILE = 512
NA_ROWS = 4
NA_BAND = NA_ROWS + WIN_H
Q_BLOCK = 128
GQA_BAND = Q_BLOCK + 2 * WINDOW
VMEM_LIMIT = 56 * 1024 * 1024

BF16 = jnp.bfloat16
F32 = jnp.float32


def _dot(a, b):
    return jnp.dot(a, b, preferred_element_type=F32)


def _dot_nt(a, b):
    return lax.dot_general(a, b, (((1,), (1,)), ((), ())), preferred_element_type=F32)


def _rmsnorm_rows(x, g):
    return x * lax.rsqrt(jnp.mean(x * x, axis=-1, keepdims=True) + EPS) * g


def _lane_is_low(shape=(1, LANES)):
    return lax.broadcasted_iota(jnp.int32, shape, len(shape) - 1) < HEAD_DIM


def _const_spec(shape):
    return pl.BlockSpec(shape, lambda *_: (0,) * len(shape), pipeline_mode=pl.Buffered(1))


def _params(semantics):
    return pltpu.CompilerParams(dimension_semantics=semantics, vmem_limit_bytes=VMEM_LIMIT)


def _ada_kernel(c_ref, w_ref, b_ref, o_ref):
    c = c_ref[...]
    s = (c / (1.0 + jnp.exp(-c))).astype(BF16)
    o_ref[...] = _dot(s, w_ref[...].astype(BF16)) + b_ref[...]


def _ada_mods(cond, ada_w, ada_b):
    depth, _, n_out = ada_w.shape
    nb = n_out // D_MODEL
    return pl.pallas_call(
        _ada_kernel,
        grid=(depth, nb),
        in_specs=[
            pl.BlockSpec((COND_ROWS, D_MODEL), lambda l, n: (0, 0)),
            pl.BlockSpec((None, D_MODEL, D_MODEL), lambda l, n: (l, 0, n)),
            pl.BlockSpec((None, 1, D_MODEL), lambda l, n: (l, 0, n)),
        ],
        out_specs=pl.BlockSpec((None, COND_ROWS, D_MODEL), lambda l, n: (l, 0, n)),
        out_shape=jax.ShapeDtypeStruct((depth, COND_ROWS, n_out), F32),
        compiler_params=_params(("arbitrary", "arbitrary")),
        name="ada_mods",
    )(cond, ada_w, ada_b.reshape(depth, 1, n_out))


def _head_rmsnorm(blk, gain):
    sq = blk * blk
    hi = sq.astype(BF16)
    lo = (sq - hi.astype(F32)).astype(BF16)
    r = lax.shift_right_logical(lax.broadcasted_iota(jnp.int32, (LANES, LANES), 0), HEAD_SHIFT)
    c = lax.shift_right_logical(lax.broadcasted_iota(jnp.int32, (LANES, LANES), 1), HEAD_SHIFT)
    avg = jnp.where(r == c, 1.0 / HEAD_DIM, 0.0).astype(BF16)
    ms = _dot(hi, avg) + _dot(lo, avg)
    return blk * lax.rsqrt(ms + EPS) * gain


def _rope_block(blk, cos, sin_signed):
    lane = lax.broadcasted_iota(jnp.int32, (1, LANES), 1)
    first = (lane & 16) == 0
    partner = jnp.where(first, pltpu.roll(blk, LANES - 16, 1), pltpu.roll(blk, 16, 1))
    return blk * cos + partner * sin_signed


def _qkv_kernel(*refs, n_kv, qk_norm, rope, dup_kv):
    x_ref, mods_ref, g_ref, wq_ref, wk_ref, wv_ref = refs[:6]
    pos = 6
    if qk_norm:
        qn_ref, kn_ref = refs[pos:pos + 2]
        pos += 2
    if rope:
        cos_ref, sin_ref = refs[pos:pos + 2]
        pos += 2
    q_ref, k_ref, v_ref = refs[pos:pos + 3]
    low = _lane_is_low()
    n_sub = x_ref.shape[0] // QKV_SUB_TILE

    def project(i):
        rows = slice(i * QKV_SUB_TILE, (i + 1) * QKV_SUB_TILE)
        h = _rmsnorm_rows(x_ref[rows, :], g_ref[...]) * (1.0 + mods_ref[1:2, :]) + mods_ref[0:1, :]
        h = h.astype(BF16)
        return _dot(h, wq_ref[...]), _dot(h, wk_ref[...]), _dot(h, wv_ref[...])

    def finish(i, qkv):
        q, k, v = qkv
        rows = slice(i * QKV_SUB_TILE, (i + 1) * QKV_SUB_TILE)

        def post(blk, gain_ref):
            if qk_norm:
                blk = _head_rmsnorm(blk, gain_ref[...])
            if rope:
                blk = _rope_block(blk, cos_ref[rows, :], sin_ref[rows, :])
            return blk

        for cb in range(N_HEADS * HEAD_DIM // LANES):
            sl = slice(cb * LANES, (cb + 1) * LANES)
            blk = post(q[:, sl], qn_ref if qk_norm else None)
            q_ref[rows, sl] = (blk * Q_SCALE).astype(q_ref.dtype)

        for cb in range(n_kv * HEAD_DIM // LANES):
            sl = slice(cb * LANES, (cb + 1) * LANES)
            kb = post(k[:, sl], kn_ref if qk_norm else None)
            vb = v[:, sl]
            if dup_kv:
                for arr, ref in ((kb, k_ref), (vb, v_ref)):
                    rolled = pltpu.roll(arr, HEAD_DIM, 1)
                    ref[rows, (2 * cb) * LANES:(2 * cb + 1) * LANES] = jnp.where(low, arr, rolled).astype(ref.dtype)
                    ref[rows, (2 * cb + 1) * LANES:(2 * cb + 2) * LANES] = jnp.where(low, rolled, arr).astype(ref.dtype)
            else:
                k_ref[rows, sl] = kb.astype(k_ref.dtype)
                v_ref[rows, sl] = vb.astype(v_ref.dtype)

    nxt = project(0)
    for i in range(n_sub):
        cur, nxt = nxt, (project(i + 1) if i + 1 < n_sub else None)
        finish(i, cur)


def _qkv_ctx_kernel(*refs, n_kv, qk_norm, seq, n_prev):
    x_ref, mods_ref, g_ref, wq_ref, wkt_ref, wvt_ref = refs[:6]
    pos = 6
    if qk_norm:
        qn_ref, kn_ref = refs[pos:pos + 2]
        pos += 2
    if n_prev:
        kprev_ref, vprev_ref = refs[pos:pos + 2]
    q_ref, kt_ref, vt_ref = refs[-3:]

    h = _rmsnorm_rows(x_ref[...], g_ref[...]) * (1.0 + mods_ref[1:2, :]) + mods_ref[0:1, :]
    h = h.astype(BF16)
    q = _dot(h, wq_ref[...])
    for cb in range(N_HEADS * HEAD_DIM // LANES):
        sl = slice(cb * LANES, (cb + 1) * LANES)
        blk = _head_rmsnorm(q[:, sl], qn_ref[...]) if qk_norm else q[:, sl]
        q_ref[:, sl] = (blk * Q_SCALE).astype(q_ref.dtype)

    kt = _dot_nt(wkt_ref[...], h)
    vt = _dot_nt(wvt_ref[...], h)
    if qk_norm:
        k3 = kt.reshape(n_kv, HEAD_DIM, kt.shape[1])
        k3 = k3 * lax.rsqrt(jnp.mean(k3 * k3, axis=1, keepdims=True) + EPS)
        kt = k3.reshape(kt.shape) * kn_ref[...]
    if n_prev:
        kt_ref[:, :n_prev] = kprev_ref[...]
        vt_ref[:, :n_prev] = vprev_ref[...]
    for b in range(kt.shape[1] // seq):
        kt_ref[b, n_prev] = kt[:, b * seq:(b + 1) * seq]
        vt_ref[b, n_prev] = vt[:, b * seq:(b + 1) * seq]


def _qkv_proj_ctx(x, mods, gain, wq, wkt, wvt, *, n_kv, seq, qn=None, kn_col=None, prev=None):
    tokens = x.shape[0]
    tm = TOKEN_TILE
    steps = tokens // tm
    nk = n_kv * HEAD_DIM
    n_prev = 0 if prev is None else prev[0].shape[1]
    in_specs = [
        pl.BlockSpec((tm, D_MODEL), lambda i: (i, 0)),
        pl.BlockSpec((None, ADA_CHUNKS, D_MODEL), lambda i: (0, 0, 0)),
        _const_spec((1, D_MODEL)),
        _const_spec((D_MODEL, D_MODEL)),
        _const_spec((nk, D_MODEL)),
        _const_spec((nk, D_MODEL)),
    ]
    args = [x, mods, gain, wq, wkt, wvt]
    if qn is not None:
        in_specs += [_const_spec((1, LANES)), _const_spec((nk, 1))]
        args += [qn, kn_col]
    if n_prev:
        in_specs += [pl.BlockSpec((tm // seq, n_prev, nk, seq), lambda i: (i, 0, 0, 0))] * 2
        args += list(prev)
    kv_spec = pl.BlockSpec((tm // seq, n_prev + 1, nk, seq), lambda i: (i, 0, 0, 0))
    kv_shape = jax.ShapeDtypeStruct((tokens // seq, n_prev + 1, nk, seq), F32)
    return pl.pallas_call(
        functools.partial(_qkv_ctx_kernel, n_kv=n_kv, qk_norm=qn is not None, seq=seq, n_prev=n_prev),
        grid=(steps,),
        in_specs=in_specs,
        out_specs=[pl.BlockSpec((tm, D_MODEL), lambda i: (i, 0)), kv_spec, kv_spec],
        out_shape=[jax.ShapeDtypeStruct((tokens, D_MODEL), BF16), kv_shape, kv_shape],
        compiler_params=_params(("arbitrary",)),
        name="qkv_proj_ctx",
    )(*args)


def _qkv_proj(x, mods, gain, wq, wk, wv, *, n_kv, kv_dtype, dup_kv, qn=None, kn=None, rope=None, seq=None):
    tokens = x.shape[0]
    tm = QKV_TILE
    steps = tokens // tm
    tiles_per_cond = steps // mods.shape[0]
    nk = n_kv * HEAD_DIM
    nk_out = 2 * nk if dup_kv else nk
    in_specs = [
        pl.BlockSpec((tm, D_MODEL), lambda i: (i, 0)),
        pl.BlockSpec((None, ADA_CHUNKS, D_MODEL), lambda i: (i // tiles_per_cond, 0, 0)),
        _const_spec((1, D_MODEL)),
        _const_spec((D_MODEL, D_MODEL)),
        _const_spec((D_MODEL, nk)),
        _const_spec((D_MODEL, nk)),
    ]
    args = [x, mods, gain, wq, wk, wv]
    if qn is not None:
        in_specs += [_const_spec((1, LANES)), _const_spec((1, LANES))]
        args += [qn, kn]
    if rope is not None:
        tiles_per_seq = seq // tm
        in_specs += [pl.BlockSpec((tm, LANES), lambda i: (i % tiles_per_seq, 0))] * 2
        args += list(rope)
    return pl.pallas_call(
        functools.partial(_qkv_kernel, n_kv=n_kv, qk_norm=qn is not None, rope=rope is not None, dup_kv=dup_kv),
        grid=(steps,),
        in_specs=in_specs,
        out_specs=[
            pl.BlockSpec((tm, D_MODEL), lambda i: (i, 0)),
            pl.BlockSpec((tm, nk_out), lambda i: (i, 0)),
            pl.BlockSpec((tm, nk_out), lambda i: (i, 0)),
        ],
        out_shape=[
            jax.ShapeDtypeStruct((tokens, D_MODEL), BF16),
            jax.ShapeDtypeStruct((tokens, nk_out), kv_dtype),
            jax.ShapeDtypeStruct((tokens, nk_out), kv_dtype),
        ],
        compiler_params=_params(("arbitrary",)),
        name="qkv_proj",
    )(*args)


def _mlp_kernel(*refs, final):
    x_ref, o_ref, mods_ref, wo_ref, g_ref, w1_ref, b1_ref, w2_ref, b2_ref = refs[:9]
    fg_ref = refs[9] if final else None
    out_ref = refs[-1]

    x1 = x_ref[...] + mods_ref[2:3, :] * _dot(o_ref[...], wo_ref[...])
    h = _rmsnorm_rows(x1, g_ref[...]) * (1.0 + mods_ref[4:5, :]) + mods_ref[3:4, :]
    h = h.astype(BF16)
    acc = jnp.zeros_like(x1)
    for c in range(D_FF // MLP_CHUNK):
        sl = slice(c * MLP_CHUNK, (c + 1) * MLP_CHUNK)
        t = jnp.maximum(_dot(h, w1_ref[:, sl]) + b1_ref[:, sl], 0.0)
        acc = acc + _dot((t * t).astype(BF16), w2_ref[sl, :])
    x2 = x1 + mods_ref[5:6, :] * (acc + b2_ref[...])
    if final:
        x2 = _rmsnorm_rows(x2, fg_ref[...])
    out_ref[...] = x2


def _mlp_block(x, o, mods, wo, gain, w1, b1, w2, b2, final_gain=None):
    tokens = x.shape[0]
    tm = TOKEN_TILE
    steps = tokens // tm
    tiles_per_cond = steps // mods.shape[0]
    in_specs = [
        pl.BlockSpec((tm, D_MODEL), lambda i: (i, 0)),
        pl.BlockSpec((tm, D_MODEL), lambda i: (i, 0)),
        pl.BlockSpec((None, ADA_CHUNKS, D_MODEL), lambda i: (i // tiles_per_cond, 0, 0)),
        _const_spec((D_MODEL, D_MODEL)),
        _const_spec((1, D_MODEL)),
        _const_spec((D_MODEL, D_FF)),
        _const_spec((1, D_FF)),
        _const_spec((D_FF, D_MODEL)),
        _const_spec((1, D_MODEL)),
    ]
    args = [x, o, mods, wo, gain, w1, b1, w2, b2]
    if final_gain is not None:
        in_specs.append(_const_spec((1, D_MODEL)))
        args.append(final_gain)
    return pl.pallas_call(
        functools.partial(_mlp_kernel, final=final_gain is not None),
        grid=(steps,),
        in_specs=in_specs,
        out_specs=pl.BlockSpec((tm, D_MODEL), lambda i: (i, 0)),
        out_shape=jax.ShapeDtypeStruct((tokens, D_MODEL), F32),
        compiler_params=_params(("arbitrary",)),
        name="mlp_block",
    )(*args)


def _softmax_numerators(scores, extra_logit=None):
    m = functools.reduce(jnp.maximum, [jnp.max(s, axis=-1, keepdims=True) for s in scores])
    if extra_logit is not None:
        m = jnp.maximum(m, extra_logit)
    ps = [jnp.exp2(s - m).astype(BF16) for s in scores]
    return ps, (None if extra_logit is None else jnp.exp2(extra_logit - m))


def _with_ones(vals, value_lanes):
    return jnp.where(value_lanes, vals, jnp.ones_like(vals))


def _dup_head(blk, keep):
    return jnp.where(keep, blk, pltpu.roll(blk, HEAD_DIM, 1))


def _stack_heads(qb, low):
    zero = jnp.zeros_like(qb)
    return jnp.concatenate([jnp.where(low, qb, zero), jnp.where(jnp.logical_not(low), qb, zero)], axis=0)


def _values_with_ones(vb, low):
    return jnp.concatenate([_with_ones(vb, low), _with_ones(vb, jnp.logical_not(low))], axis=1)


def _stacked_pv(ps, v4s, low, extra=None, values_transposed=False):
    high = jnp.logical_not(low)
    mm = _dot_nt if values_transposed else _dot
    acc = functools.reduce(jnp.add, [mm(p, v) for p, v in zip(ps, v4s)])
    half = acc.shape[0] // 2
    outs = []
    for rows, lanes, ones_lanes in ((slice(0, half), slice(0, LANES), high), (slice(half, None), slice(LANES, None), low)):
        part = acc[rows, lanes]
        if extra is not None:
            part = part + jnp.where(ones_lanes, extra[rows], 0.0)
        denom = jnp.max(jnp.where(ones_lanes, part, 0.0), axis=-1, keepdims=True)
        outs.append(part * (1.0 / denom))
    return outs


def _ctx_attn_kernel(*refs, n_kv, has_sink):
    if has_sink:
        sink_ref, q_ref, kt_ref, vt_ref, o_ref = refs
    else:
        q_ref, kt_ref, vt_ref, o_ref = refs
    low = _lane_is_low()
    seq = q_ref.shape[0]
    n_pairs = N_HEADS // 2
    ones = jnp.ones((HEAD_DIM, seq), BF16)

    def head_rows(h):
        kvh = h * n_kv // N_HEADS
        return slice(kvh * HEAD_DIM, (kvh + 1) * HEAD_DIM)

    def scores_of(p):
        kt = jnp.concatenate([kt_ref[head_rows(2 * p), :], kt_ref[head_rows(2 * p + 1), :]], axis=0)
        return [_dot(_stack_heads(q_ref[:, p * LANES:(p + 1) * LANES], low), kt.astype(BF16))]

    nxt = scores_of(0)
    for p in range(n_pairs):
        cur, nxt = nxt, (scores_of(p + 1) if p + 1 < n_pairs else None)
        sink_col = None
        if has_sink:
            first_head = lax.broadcasted_iota(jnp.int32, (2 * seq, 1), 0) < seq
            sink_col = jnp.where(first_head, sink_ref[2 * p], sink_ref[2 * p + 1]) * LOG2E
        ps, extra = _softmax_numerators(cur, sink_col)
        v4t = jnp.concatenate([vt_ref[head_rows(2 * p), :].astype(BF16), ones, ones,
                               vt_ref[head_rows(2 * p + 1), :].astype(BF16)], axis=0)
        even, odd = _stacked_pv(ps, [v4t], low, extra, values_transposed=True)
        o_ref[:, p * LANES:(p + 1) * LANES] = jnp.where(low, even, odd).astype(o_ref.dtype)


def _ctx_attention(q, kt, vt, *, batch, seq, n_kv, sink=None):
    nk = n_kv * HEAD_DIM
    slot = kt.shape[1] - 1
    in_specs = [
        pl.BlockSpec((seq, D_MODEL), lambda b: (b, 0)),
        pl.BlockSpec((None, None, nk, seq), lambda b: (b, slot, 0, 0)),
        pl.BlockSpec((None, None, nk, seq), lambda b: (b, slot, 0, 0)),
    ]
    args = [q, kt, vt]
    if sink is not None:
        in_specs.insert(0, pl.BlockSpec(memory_space=pltpu.SMEM))
        args.insert(0, sink)
    return pl.pallas_call(
        functools.partial(_ctx_attn_kernel, n_kv=n_kv, has_sink=sink is not None),
        grid=(batch,),
        in_specs=in_specs,
        out_specs=pl.BlockSpec((seq, D_MODEL), lambda b: (b, 0)),
        out_shape=jax.ShapeDtypeStruct((batch * seq, D_MODEL), BF16),
        compiler_params=_params(("arbitrary",)),
        name="ctx_attention",
    )(*args)


def _na_group_geometry(rows):
    kh = min(WIN_H, rows)
    n_groups = rows // NA_ROWS
    masked = 2 * WIN_H - 1
    band0 = [int(np.clip(g * NA_ROWS - kh // 2, 0, rows - NA_BAND)) for g in range(n_groups)]
    table_of = [0 if g == 0 else (2 if g == n_groups - 1 else 1) for g in range(n_groups)]
    sel = np.full((3, NA_ROWS, NA_BAND), masked, np.int64)
    for g in (0, 1, n_groups - 1):
        for a in range(NA_ROWS):
            r = g * NA_ROWS + a
            r0 = int(np.clip(r - kh // 2, 0, rows - kh))
            for i in range(NA_BAND):
                kr = band0[g] + i
                if r0 <= kr < r0 + kh:
                    sel[table_of[g], a, i] = kr - r + WIN_H - 1
    return band0, table_of, sel


def _na_attn_kernel(q_ref, k_ref, v_ref, kc_ref, vc_ref, rpb_ref, o_ref, tab_scr, v4_scr, *, rows):
    low = _lane_is_low()
    band0, table_of, sel = _na_group_geometry(rows)
    q_rows = NA_ROWS * GRID_W
    k_rows = NA_BAND * GRID_W
    n_groups = rows // NA_ROWS

    @pl.when(pl.program_id(1) == 0)
    def _():
        qc = lax.broadcasted_iota(jnp.int32, (GRID_W, GRID_W), 0)
        kc = lax.broadcasted_iota(jnp.int32, (GRID_W, GRID_W), 1)
        c0 = jnp.clip(qc - WIN_W // 2, 0, GRID_W - WIN_W)
        col_valid = (kc >= c0) & (kc < c0 + WIN_W)
        for half in range(2):
            blocks = []
            for ri in range(2 * WIN_H - 1):
                row = jnp.broadcast_to(rpb_ref[half, ri:ri + 1, :], (GRID_W, 2 * GRID_W))
                skew = pltpu.roll(row, GRID_W + 1, 1, stride=1, stride_axis=0)
                blocks.append(jnp.where(col_valid, skew[:, :GRID_W], NEG_INF))
            blocks.append(jnp.full((GRID_W, GRID_W), NEG_INF, F32))
            for t in range(3):
                for a in range(NA_ROWS):
                    strip = [blocks[int(sel[t, a, i])] for i in range(NA_BAND)]
                    r = half * q_rows + a * GRID_W
                    tab_scr[t, r:r + GRID_W, :] = jnp.concatenate(strip, axis=1)

    kcb = kc_ref[...].astype(BF16)
    vc4 = _values_with_ones(vc_ref[...].astype(BF16), low)
    v4_scr[...] = _values_with_ones(v_ref[...], low)

    def scores_of(g):
        rq = g * q_rows
        ks = band0[g] * GRID_W
        qs = _stack_heads(q_ref[rq:rq + q_rows, :], low)
        return [_dot_nt(qs, k_ref[ks:ks + k_rows, :]) + tab_scr[table_of[g]], _dot_nt(qs, kcb)]

    nxt = scores_of(0)
    for g in range(n_groups):
        cur, nxt = nxt, (scores_of(g + 1) if g + 1 < n_groups else None)
        ks = band0[g] * GRID_W
        ps, _ = _softmax_numerators(cur)
        lo, hi = _stacked_pv(ps, [v4_scr[ks:ks + k_rows, :], vc4], low)
        o_ref[g * q_rows:(g + 1) * q_rows, :] = jnp.where(low, lo, hi).astype(o_ref.dtype)


def _na_padded_rpb(rpb):
    pad_l = GRID_W - WIN_W
    return jnp.pad(rpb * LOG2E, ((0, 0), (0, 1), (pad_l, 2 * GRID_W - pad_l - rpb.shape[2])))


def _na_attention(q, k, v, cache_k, cache_v, rpb_pad, *, batch, seq, layer_slot):
    rows = seq // GRID_W
    n_pairs = N_HEADS // 2
    q3, k3, v3 = (a.reshape(batch, seq, D_MODEL) for a in (q, k, v))
    past = cache_k.shape[2]
    ck = cache_k.reshape(batch, cache_k.shape[1], past, D_MODEL)
    cv = cache_v.reshape(batch, cache_v.shape[1], past, D_MODEL)
    tok_spec = pl.BlockSpec((None, seq, LANES), lambda p, b: (b, 0, p))
    ctx_spec = pl.BlockSpec((None, None, past, LANES), lambda p, b: (b, layer_slot, 0, p))
    out = pl.pallas_call(
        functools.partial(_na_attn_kernel, rows=rows),
        grid=(n_pairs, batch),
        in_specs=[tok_spec, tok_spec, tok_spec, ctx_spec, ctx_spec,
                  pl.BlockSpec((2,) + rpb_pad.shape[1:], lambda p, b: (p, 0, 0))],
        out_specs=tok_spec,
        out_shape=jax.ShapeDtypeStruct((batch, seq, D_MODEL), BF16),
        scratch_shapes=[pltpu.VMEM((3, 2 * NA_ROWS * GRID_W, NA_BAND * GRID_W), F32),
                        pltpu.VMEM((seq, 2 * LANES), BF16)],
        compiler_params=_params(("arbitrary", "arbitrary")),
        name="na_attention",
    )(q3, k3, v3, ck, cv, rpb_pad)
    return out.reshape(batch * seq, D_MODEL)


def _gqa_attn_kernel(*refs, seq, windowed):
    if windowed:
        sink_ref, q_ref, k_ref, v_ref, kc_ref, vc_ref, o_ref, v4_scr, bias_scr = refs
    else:
        q_ref, k_ref, v_ref, kc_ref, vc_ref, o_ref, v4_scr = refs
    kvh = pl.program_id(1)
    lane_half = lax.shift_right_logical(lax.broadcasted_iota(jnp.int32, (1, LANES), 1), HEAD_SHIFT)
    keep = lane_half == (kvh & 1)
    low = lane_half == 0
    high = jnp.logical_not(low)
    band = GQA_BAND
    stacked = GQA_GROUP * Q_BLOCK
    n_blocks = seq // Q_BLOCK

    head_order = list(range(0, GQA_GROUP, 2)) + list(range(1, GQA_GROUP, 2))

    kcb = _dup_head(kc_ref[...], keep).astype(BF16)
    vc4 = _values_with_ones(_dup_head(vc_ref[...], keep).astype(BF16), low)
    v4_scr[...] = _values_with_ones(v_ref[...], low)

    if windowed:
        q_shift = Q_BLOCK.bit_length() - 1
        block_of_row = lax.shift_right_logical(lax.broadcasted_iota(jnp.int32, (stacked, 1), 0), q_shift)
        sink_col = jnp.zeros((stacked, 1), F32)
        for blk_i, g in enumerate(head_order):
            sink_col = jnp.where(block_of_row == blk_i, sink_ref[kvh * GQA_GROUP + g] * LOG2E, sink_col)

        @pl.when((pl.program_id(0) == 0) & (kvh == 0))
        def _():
            q_off = lax.broadcasted_iota(jnp.int32, (stacked, band), 0) & (Q_BLOCK - 1)
            k_off = lax.broadcasted_iota(jnp.int32, (stacked, band), 1)
            for t in range(bias_scr.shape[0]):
                dist = q_off - k_off + t * Q_BLOCK
                bias_scr[t] = jnp.where(jnp.abs(dist) <= WINDOW, 0.0, NEG_INF)

    def key_rows(i):
        if not windowed:
            return slice(None)
        ks = int(np.clip(i * Q_BLOCK - WINDOW, 0, seq - band))
        return slice(ks, ks + band)

    def scores_of(i):
        r0 = i * Q_BLOCK
        qblk = q_ref[r0:r0 + Q_BLOCK, :]
        zero = jnp.zeros((Q_BLOCK, LANES), qblk.dtype)
        parts = []
        for g in head_order:
            blk = qblk[:, (g // 2) * LANES:(g // 2 + 1) * LANES]
            parts.append(jnp.where(low if g % 2 == 0 else high, blk, zero))
        qs = jnp.concatenate(parts, axis=0)
        keys = key_rows(i)
        s_lat = _dot_nt(qs, k_ref[keys, :])
        if windowed:
            s_lat = s_lat + bias_scr[(r0 - keys.start) // Q_BLOCK]
        return [s_lat, _dot_nt(qs, kcb)]

    nxt = scores_of(0)
    for i in range(n_blocks):
        cur, nxt = nxt, (scores_of(i + 1) if i + 1 < n_blocks else None)
        ps, extra = _softmax_numerators(cur, sink_col if windowed else None)
        even, odd = _stacked_pv(ps, [v4_scr[key_rows(i), :], vc4], low, extra)
        r0 = i * Q_BLOCK
        for c in range(GQA_GROUP // 2):
            rows_c = slice(c * Q_BLOCK, (c + 1) * Q_BLOCK)
            o_ref[r0:r0 + Q_BLOCK, c * LANES:(c + 1) * LANES] = jnp.where(low, even[rows_c], odd[rows_c]).astype(o_ref.dtype)


def _gqa_attention(q, kd, vd, cache_k, cache_v, *, batch, seq, layer_slot, sink=None):
    group_w = GQA_GROUP * HEAD_DIM
    q3 = q.reshape(batch, seq, D_MODEL)
    kd3 = kd.reshape(batch, seq, N_KV_GQA * LANES)
    vd3 = vd.reshape(batch, seq, N_KV_GQA * LANES)
    past = cache_k.shape[2]
    ck = cache_k.reshape(batch, cache_k.shape[1], past, N_KV_GQA * HEAD_DIM)
    cv = cache_v.reshape(batch, cache_v.shape[1], past, N_KV_GQA * HEAD_DIM)
    q_spec = pl.BlockSpec((None, seq, group_w), lambda b, h: (b, 0, h))
    kv_spec = pl.BlockSpec((None, seq, LANES), lambda b, h: (b, 0, h))
    ctx_spec = pl.BlockSpec((None, None, past, LANES), lambda b, h: (b, layer_slot, 0, h // 2))
    in_specs = [q_spec, kv_spec, kv_spec, ctx_spec, ctx_spec]
    args = [q3, kd3, vd3, ck, cv]
    scratch = [pltpu.VMEM((seq, 2 * LANES), BF16)]
    if sink is not None:
        in_specs.insert(0, pl.BlockSpec(memory_space=pltpu.SMEM))
        args.insert(0, sink)
        n_offsets = (GQA_BAND - Q_BLOCK) // Q_BLOCK + 1
        scratch.append(pltpu.VMEM((n_offsets, GQA_GROUP * Q_BLOCK, GQA_BAND), F32))
    out = pl.pallas_call(
        functools.partial(_gqa_attn_kernel, seq=seq, windowed=sink is not None),
        grid=(batch, N_KV_GQA),
        in_specs=in_specs,
        out_specs=q_spec,
        out_shape=jax.ShapeDtypeStruct((batch, seq, D_MODEL), BF16),
        scratch_shapes=scratch,
        compiler_params=_params(("arbitrary", "arbitrary")),
        name="gqa_attention",
    )(*args)
    return out.reshape(batch * seq, D_MODEL)


def _rope_tables(seq):
    t = jnp.arange(seq)
    rows = (t // GRID_W).astype(F32)
    cols = (t % GRID_W).astype(F32)
    quarter = HEAD_DIM // 4
    freqs = jnp.exp(-math.log(ROPE_BASE) * jnp.arange(quarter, dtype=F32) / quarter)
    ang_r = rows[:, None] * freqs[None, :]
    ang_c = cols[:, None] * freqs[None, :]
    cos = jnp.concatenate([jnp.cos(ang_r)] * 2 + [jnp.cos(ang_c)] * 2, axis=1)
    sin = jnp.concatenate([-jnp.sin(ang_r), jnp.sin(ang_r), -jnp.sin(ang_c), jnp.sin(ang_c)], axis=1)
    return jnp.tile(cos, (1, LANES // HEAD_DIM)), jnp.tile(sin, (1, LANES // HEAD_DIM))


def kernel(x_prompt, x_sample, cache_k_a, cache_v_a, cache_k_b, cache_v_b, cache_k_c, cache_v_c, c, c_ctx,
           ada_w, ada_b, norm_mix_g, norm_mlp_g, w_o, mlp_w1, mlp_b1, mlp_w2, mlp_b2, w_qkv_a, rpb_a,
           w_qkv_b, sink_b, w_qkv_c, q_norm_c, k_norm_c, final_norm_g):
    batch, seq, _ = x_prompt.shape
    dec_batch, dec_seq, _ = x_sample.shape
    depth = ada_w.shape[0]
    assert dec_batch + 1 <= COND_ROWS and dec_seq % GRID_W == 0
    assert (dec_seq // GRID_W) % NA_ROWS == 0 and dec_seq // GRID_W >= 3 * NA_ROWS

    cond = jnp.concatenate([c, c_ctx[None, :], jnp.zeros((COND_ROWS - dec_batch - 1, D_MODEL), F32)], axis=0)
    mods = _ada_mods(cond, ada_w, ada_b).reshape(depth, COND_ROWS, ADA_CHUNKS, D_MODEL)

    w_qkv = (w_qkv_a, w_qkv_b, w_qkv_c)
    n_kv = (N_KV_A, N_KV_GQA, N_KV_GQA)
    caches_k = (cache_k_a, cache_k_b, cache_k_c)
    caches_v = (cache_v_a, cache_v_b, cache_v_c)
    new_kv = [None] * N_MIXERS
    rope = _rope_tables(dec_seq)
    nq = N_HEADS * HEAD_DIM

    xp = x_prompt.reshape(batch * seq, D_MODEL)
    xs = x_sample.reshape(dec_batch * dec_seq, D_MODEL)

    for l in range(depth):
        m, j = l % N_MIXERS, l // N_MIXERS
        nk = n_kv[m] * HEAD_DIM
        w = w_qkv[m][j].astype(BF16)
        wq, wk, wv = w[:, :nq], w[:, nq:nq + nk], w[:, nq + nk:]
        wo = w_o[l].astype(BF16)
        w1 = mlp_w1[l].astype(BF16)
        w2 = mlp_w2[l].astype(BF16)
        b1 = mlp_b1[l][None, :]
        b2 = mlp_b2[l][None, :]
        g_mix = norm_mix_g[l][None, :]
        g_mlp = norm_mlp_g[l][None, :]
        mods_ctx = mods[l, dec_batch:dec_batch + 1]
        mods_lat = mods[l, :dec_batch]
        final = final_norm_g[None, :] if l == depth - 1 else None
        norm_args, ctx_norm_args = {}, {}
        if m == 2:
            qn = jnp.tile(q_norm_c[j], LANES // HEAD_DIM)[None, :]
            norm_args = dict(qn=qn, kn=jnp.tile(k_norm_c[j], LANES // HEAD_DIM)[None, :])
            ctx_norm_args = dict(qn=qn, kn_col=jnp.tile(k_norm_c[j], n_kv[m])[:, None])
        sink = sink_b[j] if m == 1 else None

        q, kt, vt = _qkv_proj_ctx(xp, mods_ctx, g_mix, wq, wk.T, wv.T, n_kv=n_kv[m], seq=seq, prev=new_kv[m],
                                  **ctx_norm_args)
        o = _ctx_attention(q, kt, vt, batch=batch, seq=seq, n_kv=n_kv[m], sink=sink)
        xp = _mlp_block(xp, o, mods_ctx, wo, g_mlp, w1, b1, w2, b2, final)
        new_kv[m] = (kt, vt)

        if m == 0:
            q, k, v = _qkv_proj(xs, mods_lat, g_mix, wq, wk, wv, n_kv=n_kv[m], kv_dtype=BF16, dup_kv=False)
            o = _na_attention(q, k, v, caches_k[m], caches_v[m], _na_padded_rpb(rpb_a[j]),
                              batch=dec_batch, seq=dec_seq, layer_slot=j)
        else:
            q, k, v = _qkv_proj(xs, mods_lat, g_mix, wq, wk, wv, n_kv=n_kv[m], kv_dtype=BF16, dup_kv=True,
                                rope=rope, seq=dec_seq, **norm_args)
            o = _gqa_attention(q, k, v, caches_k[m], caches_v[m], batch=dec_batch, seq=dec_seq, layer_slot=j, sink=sink)
        xs = _mlp_block(xs, o, mods_lat, wo, g_mlp, w1, b1, w2, b2, final)

    y_prompt = xp.reshape(batch, seq, D_MODEL)
    y_sample = xs.reshape(dec_batch, dec_seq, D_MODEL)
    outs = [y_prompt, y_sample]
    for m in range(N_MIXERS):
        for stacked in new_kv[m]:
            t = stacked.reshape(batch, stacked.shape[1], n_kv[m], HEAD_DIM, seq)
            outs.append(jnp.transpose(t, (0, 1, 4, 2, 3)))
    return tuple(outs)
```

```python
import functools
import math

import numpy as np
import jax
import jax.numpy as jnp
from jax import lax
from jax.experimental import pallas as pl
from jax.experimental.pallas import tpu as pltpu

D_MODEL = 1024
N_HEADS = 16
HEAD_DIM = 64
N_KV_A = 16
N_KV_GQA = 4
GQA_GROUP = N_HEADS // N_KV_GQA
D_FF = 4 * D_MODEL
ADA_CHUNKS = 6
N_MIXERS = 3
GRID_W = 64
WINDOW = 128
WIN_H = 8
WIN_W = 16
ROPE_BASE = 10000.0
EPS = 1e-6
NEG_INF = -1e30
LOG2E = math.log2(math.e)
Q_SCALE = HEAD_DIM ** -0.5 * LOG2E

HEAD_SHIFT = HEAD_DIM.bit_length() - 1
LANES = 128
COND_ROWS = 16
TOKEN_TILE = 512
MLP_CHUNK = 1024
QKV_TILE = 1024
QKV_SUB_TILE = 512
NA_ROWS = 4
NA_BAND = NA_ROWS + WIN_H
Q_BLOCK = 128
GQA_BAND = Q_BLOCK + 2 * WINDOW
VMEM_LIMIT = 56 * 1024 * 1024

BF16 = jnp.bfloat16
F32 = jnp.float32


def _dot(a, b):
    return jnp.dot(a, b, preferred_element_type=F32)


def _dot_nt(a, b):
    return lax.dot_general(a, b, (((1,), (1,)), ((), ())), preferred_element_type=F32)


def _rmsnorm_rows(x, g):
    return x * lax.rsqrt(jnp.mean(x * x, axis=-1, keepdims=True) + EPS) * g


def _lane_is_low(shape=(1, LANES)):
    return lax.broadcasted_iota(jnp.int32, shape, len(shape) - 1) < HEAD_DIM


def _const_spec(shape):
    return pl.BlockSpec(shape, lambda *_: (0,) * len(shape), pipeline_mode=pl.Buffered(1))


def _params(semantics):
    return pltpu.CompilerParams(dimension_semantics=semantics, vmem_limit_bytes=VMEM_LIMIT)


def _ada_kernel(c_ref, w_ref, b_ref, o_ref):
    c = c_ref[...]
    s = (c / (1.0 + jnp.exp(-c))).astype(BF16)
    o_ref[...] = _dot(s, w_ref[...].astype(BF16)) + b_ref[...]


def _ada_mods(cond, ada_w, ada_b):
    depth, _, n_out = ada_w.shape
    nb = n_out // D_MODEL
    return pl.pallas_call(
        _ada_kernel,
        grid=(depth, nb),
        in_specs=[
            pl.BlockSpec((COND_ROWS, D_MODEL), lambda l, n: (0, 0)),
            pl.BlockSpec((None, D_MODEL, D_MODEL), lambda l, n: (l, 0, n)),
            pl.BlockSpec((None, 1, D_MODEL), lambda l, n: (l, 0, n)),
        ],
        out_specs=pl.BlockSpec((None, COND_ROWS, D_MODEL), lambda l, n: (l, 0, n)),
        out_shape=jax.ShapeDtypeStruct((depth, COND_ROWS, n_out), F32),
        compiler_params=_params(("arbitrary", "arbitrary")),
        name="ada_mods",
    )(cond, ada_w, ada_b.reshape(depth, 1, n_out))


def _head_rmsnorm(blk, gain):
    sq = blk * blk
    hi = sq.astype(BF16)
    lo = (sq - hi.astype(F32)).astype(BF16)
    r = lax.shift_right_logical(lax.broadcasted_iota(jnp.int32, (LANES, LANES), 0), HEAD_SHIFT)
    c = lax.shift_right_logical(lax.broadcasted_iota(jnp.int32, (LANES, LANES), 1), HEAD_SHIFT)
    avg = jnp.where(r == c, 1.0 / HEAD_DIM, 0.0).astype(BF16)
    ms = _dot(hi, avg) + _dot(lo, avg)
    return blk * lax.rsqrt(ms + EPS) * gain


def _rope_block(blk, cos, sin_signed):
    lane = lax.broadcasted_iota(jnp.int32, (1, LANES), 1)
    first = (lane & 16) == 0
    partner = jnp.where(first, pltpu.roll(blk, LANES - 16, 1), pltpu.roll(blk, 16, 1))
    return blk * cos + partner * sin_signed


def _qkv_kernel(*refs, n_kv, qk_norm, rope, dup_kv):
    x_ref, mods_ref, g_ref, wq_ref, wk_ref, wv_ref = refs[:6]
    pos = 6
    if qk_norm:
        qn_ref, kn_ref = refs[pos:pos + 2]
        pos += 2
    if rope:
        cos_ref, sin_ref = refs[pos:pos + 2]
        pos += 2
    q_ref, k_ref, v_ref = refs[pos:pos + 3]
    low = _lane_is_low()
    n_sub = x_ref.shape[0] // QKV_SUB_TILE

    def project(i):
        rows = slice(i * QKV_SUB_TILE, (i + 1) * QKV_SUB_TILE)
        h = _rmsnorm_rows(x_ref[rows, :], g_ref[...]) * (1.0 + mods_ref[1:2, :]) + mods_ref[0:1, :]
        h = h.astype(BF16)
        return _dot(h, wq_ref[...]), _dot(h, wk_ref[...]), _dot(h, wv_ref[...])

    def finish(i, qkv):
        q, k, v = qkv
        rows = slice(i * QKV_SUB_TILE, (i + 1) * QKV_SUB_TILE)

        def post(blk, gain_ref):
            if qk_norm:
                blk = _head_rmsnorm(blk, gain_ref[...])
            if rope:
                blk = _rope_block(blk, cos_ref[rows, :], sin_ref[rows, :])
            return blk

        for cb in range(N_HEADS * HEAD_DIM // LANES):
            sl = slice(cb * LANES, (cb + 1) * LANES)
            blk = post(q[:, sl], qn_ref if qk_norm else None)
            q_ref[rows, sl] = (blk * Q_SCALE).astype(q_ref.dtype)

        for cb in range(n_kv * HEAD_DIM // LANES):
            sl = slice(cb * LANES, (cb + 1) * LANES)
            kb = post(k[:, sl], kn_ref if qk_norm else None)
            vb = v[:, sl]
            if dup_kv:
                for arr, ref in ((kb, k_ref), (vb, v_ref)):
                    rolled = pltpu.roll(arr, HEAD_DIM, 1)
                    ref[rows, (2 * cb) * LANES:(2 * cb + 1) * LANES] = jnp.where(low, arr, rolled).astype(ref.dtype)
                    ref[rows, (2 * cb + 1) * LANES:(2 * cb + 2) * LANES] = jnp.where(low, rolled, arr).astype(ref.dtype)
            else:
                k_ref[rows, sl] = kb.astype(k_ref.dtype)
                v_ref[rows, sl] = vb.astype(v_ref.dtype)

    nxt = project(0)
    for i in range(n_sub):
        cur, nxt = nxt, (project(i + 1) if i + 1 < n_sub else None)
        finish(i, cur)


def _ctx_front_kernel(*refs, n_kv, qk_norm, seq, n_prev, has_sink):
    if has_sink:
        sink_ref, refs = refs[0], refs[1:]
    x_ref, mods_ref, g_ref, wq_ref, wkt_ref, wvt_ref = refs[:6]
    pos = 6
    if qk_norm:
        qn_ref, kn_ref = refs[pos:pos + 2]
        pos += 2
    if n_prev:
        kprev_ref, vprev_ref = refs[pos:pos + 2]
    o_ref, kt_ref, vt_ref = refs[-3:]
    low = _lane_is_low()
    n_pairs = N_HEADS // 2
    n_sub = x_ref.shape[0] // seq
    ones = jnp.ones((HEAD_DIM, seq), BF16)

    if n_prev:
        kt_ref[:, :n_prev] = kprev_ref[...]
        vt_ref[:, :n_prev] = vprev_ref[...]

    def head_rows(h):
        kvh = h * n_kv // N_HEADS
        return slice(kvh * HEAD_DIM, (kvh + 1) * HEAD_DIM)

    def project(b):
        rows = slice(b * seq, (b + 1) * seq)
        h = _rmsnorm_rows(x_ref[rows, :], g_ref[...]) * (1.0 + mods_ref[1:2, :]) + mods_ref[0:1, :]
        h = h.astype(BF16)
        q = _dot(h, wq_ref[...])
        q_pairs = []
        for cb in range(n_pairs):
            blk = q[:, cb * LANES:(cb + 1) * LANES]
            if qk_norm:
                blk = _head_rmsnorm(blk, qn_ref[...])
            q_pairs.append((blk * Q_SCALE).astype(BF16))
        kt = _dot_nt(wkt_ref[...], h)
        vt = _dot_nt(wvt_ref[...], h)
        if qk_norm:
            k3 = kt.reshape(n_kv, HEAD_DIM, seq)
            k3 = k3 * lax.rsqrt(jnp.mean(k3 * k3, axis=1, keepdims=True) + EPS)
            kt = k3.reshape(kt.shape) * kn_ref[...]
        kt_ref[b, n_prev] = kt
        vt_ref[b, n_prev] = vt
        return q_pairs, kt.astype(BF16), vt.astype(BF16)

    def attend(b, projected):
        q_pairs, kt, vt = projected
        rows = slice(b * seq, (b + 1) * seq)

        def scores_of(p):
            ktp = jnp.concatenate([kt[head_rows(2 * p), :], kt[head_rows(2 * p + 1), :]], axis=0)
            return [_dot(_stack_heads(q_pairs[p], low), ktp)]

        nxt = scores_of(0)
        for p in range(n_pairs):
            cur, nxt = nxt, (scores_of(p + 1) if p + 1 < n_pairs else None)
            sink_col = None
            if has_sink:
                first_head = lax.broadcasted_iota(jnp.int32, (2 * seq, 1), 0) < seq
                sink_col = jnp.where(first_head, sink_ref[2 * p], sink_ref[2 * p + 1]) * LOG2E
            ps, extra = _softmax_numerators(cur, sink_col)
            v4t = jnp.concatenate([vt[head_rows(2 * p), :], ones, ones, vt[head_rows(2 * p + 1), :]], axis=0)
            even, odd = _stacked_pv(ps, [v4t], low, extra, values_transposed=True)
            o_ref[rows, p * LANES:(p + 1) * LANES] = jnp.where(low, even, odd).astype(o_ref.dtype)

    nxt = project(0)
    for b in range(n_sub):
        cur, nxt = nxt, (project(b + 1) if b + 1 < n_sub else None)
        attend(b, cur)


def _ctx_front(x, mods, gain, wq, wkt, wvt, *, n_kv, seq, qn=None, kn_col=None, prev=None, sink=None):
    tokens = x.shape[0]
    tm = TOKEN_TILE
    steps = tokens // tm
    nk = n_kv * HEAD_DIM
    n_prev = 0 if prev is None else prev[0].shape[1]
    in_specs = [
        pl.BlockSpec((tm, D_MODEL), lambda i: (i, 0)),
        pl.BlockSpec((None, ADA_CHUNKS, D_MODEL), lambda i: (0, 0, 0)),
        _const_spec((1, D_MODEL)),
        _const_spec((D_MODEL, D_MODEL)),
        _const_spec((nk, D_MODEL)),
        _const_spec((nk, D_MODEL)),
    ]
    args = [x, mods, gain, wq, wkt, wvt]
    if qn is not None:
        in_specs += [_const_spec((1, LANES)), _const_spec((nk, 1))]
        args += [qn, kn_col]
    if n_prev:
        in_specs += [pl.BlockSpec((tm // seq, n_prev, nk, seq), lambda i: (i, 0, 0, 0))] * 2
        args += list(prev)
    if sink is not None:
        in_specs.insert(0, pl.BlockSpec(memory_space=pltpu.SMEM))
        args.insert(0, sink)
    kv_spec = pl.BlockSpec((tm // seq, n_prev + 1, nk, seq), lambda i: (i, 0, 0, 0))
    kv_shape = jax.ShapeDtypeStruct((tokens // seq, n_prev + 1, nk, seq), F32)
    return pl.pallas_call(
        functools.partial(_ctx_front_kernel, n_kv=n_kv, qk_norm=qn is not None, seq=seq, n_prev=n_prev,
                          has_sink=sink is not None),
        grid=(steps,),
        in_specs=in_specs,
        out_specs=[pl.BlockSpec((tm, D_MODEL), lambda i: (i, 0)), kv_spec, kv_spec],
        out_shape=[jax.ShapeDtypeStruct((tokens, D_MODEL), BF16), kv_shape, kv_shape],
        compiler_params=_params(("arbitrary",)),
        name="ctx_front",
    )(*args)


def _qkv_proj(x, mods, gain, wq, wk, wv, *, n_kv, kv_dtype, dup_kv, qn=None, kn=None, rope=None, seq=None):
    tokens = x.shape[0]
    tm = QKV_TILE
    steps = tokens // tm
    tiles_per_cond = steps // mods.shape[0]
    nk = n_kv * HEAD_DIM
    nk_out = 2 * nk if dup_kv else nk
    in_specs = [
        pl.BlockSpec((tm, D_MODEL), lambda i: (i, 0)),
        pl.BlockSpec((None, ADA_CHUNKS, D_MODEL), lambda i: (i // tiles_per_cond, 0, 0)),
        _const_spec((1, D_MODEL)),
        _const_spec((D_MODEL, D_MODEL)),
        _const_spec((D_MODEL, nk)),
        _const_spec((D_MODEL, nk)),
    ]
    args = [x, mods, gain, wq, wk, wv]
    if qn is not None:
        in_specs += [_const_spec((1, LANES)), _const_spec((1, LANES))]
        args += [qn, kn]
    if rope is not None:
        tiles_per_seq = seq // tm
        in_specs += [pl.BlockSpec((tm, LANES), lambda i: (i % tiles_per_seq, 0))] * 2
        args += list(rope)
    return pl.pallas_call(
        functools.partial(_qkv_kernel, n_kv=n_kv, qk_norm=qn is not None, rope=rope is not None, dup_kv=dup_kv),
        grid=(steps,),
        in_specs=in_specs,
        out_specs=[
            pl.BlockSpec((tm, D_MODEL), lambda i: (i, 0)),
            pl.BlockSpec((tm, nk_out), lambda i: (i, 0)),
            pl.BlockSpec((tm, nk_out), lambda i: (i, 0)),
        ],
        out_shape=[
            jax.ShapeDtypeStruct((tokens, D_MODEL), BF16),
            jax.ShapeDtypeStruct((tokens, nk_out), kv_dtype),
            jax.ShapeDtypeStruct((tokens, nk_out), kv_dtype),
        ],
        compiler_params=_params(("arbitrary",)),
        name="qkv_proj",
    )(*args)


def _mlp_kernel(*refs, final):
    x_ref, o_ref, mods_ref, wo_ref, g_ref, w1_ref, b1_ref, w2_ref, b2_ref = refs[:9]
    fg_ref = refs[9] if final else None
    out_ref = refs[-1]

    x1 = x_ref[...] + mods_ref[2:3, :] * _dot(o_ref[...], wo_ref[...])
    h = _rmsnorm_rows(x1, g_ref[...]) * (1.0 + mods_ref[4:5, :]) + mods_ref[3:4, :]
    h = h.astype(BF16)
    acc = jnp.zeros_like(x1)
    for c in range(D_FF // MLP_CHUNK):
        sl = slice(c * MLP_CHUNK, (c + 1) * MLP_CHUNK)
        t = jnp.maximum(_dot(h, w1_ref[:, sl]) + b1_ref[:, sl], 0.0)
        acc = acc + _dot((t * t).astype(BF16), w2_ref[sl, :])
    x2 = x1 + mods_ref[5:6, :] * (acc + b2_ref[...])
    if final:
        x2 = _rmsnorm_rows(x2, fg_ref[...])
    out_ref[...] = x2


def _mlp_block(x, o, mods, wo, gain, w1, b1, w2, b2, final_gain=None):
    tokens = x.shape[0]
    tm = TOKEN_TILE
    steps = tokens // tm
    tiles_per_cond = steps // mods.shape[0]
    in_specs = [
        pl.BlockSpec((tm, D_MODEL), lambda i: (i, 0)),
        pl.BlockSpec((tm, D_MODEL), lambda i: (i, 0)),
        pl.BlockSpec((None, ADA_CHUNKS, D_MODEL), lambda i: (i // tiles_per_cond, 0, 0)),
        _const_spec((D_MODEL, D_MODEL)),
        _const_spec((1, D_MODEL)),
        _const_spec((D_MODEL, D_FF)),
        _const_spec((1, D_FF)),
        _const_spec((D_FF, D_MODEL)),
        _const_spec((1, D_MODEL)),
    ]
    args = [x, o, mods, wo, gain, w1, b1, w2, b2]
    if final_gain is not None:
        in_specs.append(_const_spec((1, D_MODEL)))
        args.append(final_gain)
    return pl.pallas_call(
        functools.partial(_mlp_kernel, final=final_gain is not None),
        grid=(steps,),
        in_specs=in_specs,
        out_specs=pl.BlockSpec((tm, D_MODEL), lambda i: (i, 0)),
        out_shape=jax.ShapeDtypeStruct((tokens, D_MODEL), F32),
        compiler_params=_params(("arbitrary",)),
        name="mlp_block",
    )(*args)


def _softmax_numerators(scores, extra_logit=None):
    m = functools.reduce(jnp.maximum, [jnp.max(s, axis=-1, keepdims=True) for s in scores])
    if extra_logit is not None:
        m = jnp.maximum(m, extra_logit)
    ps = [jnp.exp2(s - m).astype(BF16) for s in scores]
    return ps, (None if extra_logit is None else jnp.exp2(extra_logit - m))


def _with_ones(vals, value_lanes):
    return jnp.where(value_lanes, vals, jnp.ones_like(vals))


def _dup_head(blk, keep):
    return jnp.where(keep, blk, pltpu.roll(blk, HEAD_DIM, 1))


def _stack_heads(qb, low):
    zero = jnp.zeros_like(qb)
    return jnp.concatenate([jnp.where(low, qb, zero), jnp.where(jnp.logical_not(low), qb, zero)], axis=0)


def _values_with_ones(vb, low):
    return jnp.concatenate([_with_ones(vb, low), _with_ones(vb, jnp.logical_not(low))], axis=1)


def _stacked_pv(ps, v4s, low, extra=None, values_transposed=False):
    high = jnp.logical_not(low)
    mm = _dot_nt if values_transposed else _dot
    acc = functools.reduce(jnp.add, [mm(p, v) for p, v in zip(ps, v4s)])
    half = acc.shape[0] // 2
    outs = []
    for rows, lanes, ones_lanes in ((slice(0, half), slice(0, LANES), high), (slice(half, None), slice(LANES, None), low)):
        part = acc[rows, lanes]
        if extra is not None:
            part = part + jnp.where(ones_lanes, extra[rows], 0.0)
        denom = jnp.max(jnp.where(ones_lanes, part, 0.0), axis=-1, keepdims=True)
        outs.append(part * (1.0 / denom))
    return outs


def _na_group_geometry(rows):
    kh = min(WIN_H, rows)
    n_groups = rows // NA_ROWS
    masked = 2 * WIN_H - 1
    band0 = [int(np.clip(g * NA_ROWS - kh // 2, 0, rows - NA_BAND)) for g in range(n_groups)]
    table_of = [0 if g == 0 else (2 if g == n_groups - 1 else 1) for g in range(n_groups)]
    sel = np.full((3, NA_ROWS, NA_BAND), masked, np.int64)
    for g in (0, 1, n_groups - 1):
        for a in range(NA_ROWS):
            r = g * NA_ROWS + a
            r0 = int(np.clip(r - kh // 2, 0, rows - kh))
            for i in range(NA_BAND):
                kr = band0[g] + i
                if r0 <= kr < r0 + kh:
                    sel[table_of[g], a, i] = kr - r + WIN_H - 1
    return band0, table_of, sel


def _na_attn_kernel(q_ref, k_ref, v_ref, kc_ref, vc_ref, rpb_ref, o_ref, tab_scr, v4_scr, *, rows):
    low = _lane_is_low()
    band0, table_of, sel = _na_group_geometry(rows)
    q_rows = NA_ROWS * GRID_W
    k_rows = NA_BAND * GRID_W
    n_groups = rows // NA_ROWS

    @pl.when(pl.program_id(1) == 0)
    def _():
        qc = lax.broadcasted_iota(jnp.int32, (GRID_W, GRID_W), 0)
        kc = lax.broadcasted_iota(jnp.int32, (GRID_W, GRID_W), 1)
        c0 = jnp.clip(qc - WIN_W // 2, 0, GRID_W - WIN_W)
        col_valid = (kc >= c0) & (kc < c0 + WIN_W)
        for half in range(2):
            blocks = []
            for ri in range(2 * WIN_H - 1):
                row = jnp.broadcast_to(rpb_ref[half, ri:ri + 1, :], (GRID_W, 2 * GRID_W))
                skew = pltpu.roll(row, GRID_W + 1, 1, stride=1, stride_axis=0)
                blocks.append(jnp.where(col_valid, skew[:, :GRID_W], NEG_INF))
            blocks.append(jnp.full((GRID_W, GRID_W), NEG_INF, F32))
            for t in range(3):
                for a in range(NA_ROWS):
                    strip = [blocks[int(sel[t, a, i])] for i in range(NA_BAND)]
                    r = half * q_rows + a * GRID_W
                    tab_scr[t, r:r + GRID_W, :] = jnp.concatenate(strip, axis=1)

    kcb = kc_ref[...].astype(BF16)
    vc4 = _values_with_ones(vc_ref[...].astype(BF16), low)
    v4_scr[...] = _values_with_ones(v_ref[...], low)

    def scores_of(g):
        rq = g * q_rows
        ks = band0[g] * GRID_W
        qs = _stack_heads(q_ref[rq:rq + q_rows, :], low)
        return [_dot_nt(qs, k_ref[ks:ks + k_rows, :]) + tab_scr[table_of[g]], _dot_nt(qs, kcb)]

    nxt = scores_of(0)
    for g in range(n_groups):
        cur, nxt = nxt, (scores_of(g + 1) if g + 1 < n_groups else None)
        ks = band0[g] * GRID_W
        ps, _ = _softmax_numerators(cur)
        lo, hi = _stacked_pv(ps, [v4_scr[ks:ks + k_rows, :], vc4], low)
        o_ref[g * q_rows:(g + 1) * q_rows, :] = jnp.where(low, lo, hi).astype(o_ref.dtype)


def _na_padded_rpb(rpb):
    pad_l = GRID_W - WIN_W
    return jnp.pad(rpb * LOG2E, ((0, 0), (0, 1), (pad_l, 2 * GRID_W - pad_l - rpb.shape[2])))


def _na_attention(q, k, v, cache_k, cache_v, rpb_pad, *, batch, seq, layer_slot):
    rows = seq // GRID_W
    n_pairs = N_HEADS // 2
    q3, k3, v3 = (a.reshape(batch, seq, D_MODEL) for a in (q, k, v))
    past = cache_k.shape[2]
    ck = cache_k.reshape(batch, cache_k.shape[1], past, D_MODEL)
    cv = cache_v.reshape(batch, cache_v.shape[1], past, D_MODEL)
    tok_spec = pl.BlockSpec((None, seq, LANES), lambda p, b: (b, 0, p))
    ctx_spec = pl.BlockSpec((None, None, past, LANES), lambda p, b: (b, layer_slot, 0, p))
    out = pl.pallas_call(
        functools.partial(_na_attn_kernel, rows=rows),
        grid=(n_pairs, batch),
        in_specs=[tok_spec, tok_spec, tok_spec, ctx_spec, ctx_spec,
                  pl.BlockSpec((2,) + rpb_pad.shape[1:], lambda p, b: (p, 0, 0))],
        out_specs=tok_spec,
        out_shape=jax.ShapeDtypeStruct((batch, seq, D_MODEL), BF16),
        scratch_shapes=[pltpu.VMEM((3, 2 * NA_ROWS * GRID_W, NA_BAND * GRID_W), F32),
                        pltpu.VMEM((seq, 2 * LANES), BF16)],
        compiler_params=_params(("arbitrary", "arbitrary")),
        name="na_attention",
    )(q3, k3, v3, ck, cv, rpb_pad)
    return out.reshape(batch * seq, D_MODEL)


def _gqa_attn_kernel(*refs, seq, windowed):
    if windowed:
        sink_ref, q_ref, k_ref, v_ref, kc_ref, vc_ref, o_ref, v4_scr, bias_scr = refs
    else:
        q_ref, k_ref, v_ref, kc_ref, vc_ref, o_ref, v4_scr = refs
    kvh = pl.program_id(1)
    lane_half = lax.shift_right_logical(lax.broadcasted_iota(jnp.int32, (1, LANES), 1), HEAD_SHIFT)
    keep = lane_half == (kvh & 1)
    low = lane_half == 0
    high = jnp.logical_not(low)
    band = GQA_BAND
    stacked = GQA_GROUP * Q_BLOCK
    n_blocks = seq // Q_BLOCK

    head_order = list(range(0, GQA_GROUP, 2)) + list(range(1, GQA_GROUP, 2))

    kcb = _dup_head(kc_ref[...], keep).astype(BF16)
    vc4 = _values_with_ones(_dup_head(vc_ref[...], keep).astype(BF16), low)
    v4_scr[...] = _values_with_ones(v_ref[...], low)

    if windowed:
        q_shift = Q_BLOCK.bit_length() - 1
        block_of_row = lax.shift_right_logical(lax.broadcasted_iota(jnp.int32, (stacked, 1), 0), q_shift)
        sink_col = jnp.zeros((stacked, 1), F32)
        for blk_i, g in enumerate(head_order):
            sink_col = jnp.where(block_of_row == blk_i, sink_ref[kvh * GQA_GROUP + g] * LOG2E, sink_col)

        @pl.when((pl.program_id(0) == 0) & (kvh == 0))
        def _():
            q_off = lax.broadcasted_iota(jnp.int32, (stacked, band), 0) & (Q_BLOCK - 1)
            k_off = lax.broadcasted_iota(jnp.int32, (stacked, band), 1)
            for t in range(bias_scr.shape[0]):
                dist = q_off - k_off + t * Q_BLOCK
                bias_scr[t] = jnp.where(jnp.abs(dist) <= WINDOW, 0.0, NEG_INF)

    def key_rows(i):
        if not windowed:
            return slice(None)
        ks = int(np.clip(i * Q_BLOCK - WINDOW, 0, seq - band))
        return slice(ks, ks + band)

    def scores_of(i):
        r0 = i * Q_BLOCK
        qblk = q_ref[r0:r0 + Q_BLOCK, :]
        zero = jnp.zeros((Q_BLOCK, LANES), qblk.dtype)
        parts = []
        for g in head_order:
            blk = qblk[:, (g // 2) * LANES:(g // 2 + 1) * LANES]
            parts.append(jnp.where(low if g % 2 == 0 else high, blk, zero))
        qs = jnp.concatenate(parts, axis=0)
        keys = key_rows(i)
        s_lat = _dot_nt(qs, k_ref[keys, :])
        if windowed:
            s_lat = s_lat + bias_scr[(r0 - keys.start) // Q_BLOCK]
        return [s_lat, _dot_nt(qs, kcb)]

    nxt = scores_of(0)
    for i in range(n_blocks):
        cur, nxt = nxt, (scores_of(i + 1) if i + 1 < n_blocks else None)
        ps, extra = _softmax_numerators(cur, sink_col if windowed else None)
        even, odd = _stacked_pv(ps, [v4_scr[key_rows(i), :], vc4], low, extra)
        r0 = i * Q_BLOCK
        for c in range(GQA_GROUP // 2):
            rows_c = slice(c * Q_BLOCK, (c + 1) * Q_BLOCK)
            o_ref[r0:r0 + Q_BLOCK, c * LANES:(c + 1) * LANES] = jnp.where(low, even[rows_c], odd[rows_c]).astype(o_ref.dtype)


def _gqa_attention(q, kd, vd, cache_k, cache_v, *, batch, seq, layer_slot, sink=None):
    group_w = GQA_GROUP * HEAD_DIM
    q3 = q.reshape(batch, seq, D_MODEL)
    kd3 = kd.reshape(batch, seq, N_KV_GQA * LANES)
    vd3 = vd.reshape(batch, seq, N_KV_GQA * LANES)
    past = cache_k.shape[2]
    ck = cache_k.reshape(batch, cache_k.shape[1], past, N_KV_GQA * HEAD_DIM)
    cv = cache_v.reshape(batch, cache_v.shape[1], past, N_KV_GQA * HEAD_DIM)
    q_spec = pl.BlockSpec((None, seq, group_w), lambda b, h: (b, 0, h))
    kv_spec = pl.BlockSpec((None, seq, LANES), lambda b, h: (b, 0, h))
    ctx_spec = pl.BlockSpec((None, None, past, LANES), lambda b, h: (b, layer_slot, 0, h // 2))
    in_specs = [q_spec, kv_spec, kv_spec, ctx_spec, ctx_spec]
    args = [q3, kd3, vd3, ck, cv]
    scratch = [pltpu.VMEM((seq, 2 * LANES), BF16)]
    if sink is not None:
        in_specs.insert(0, pl.BlockSpec(memory_space=pltpu.SMEM))
        args.insert(0, sink)
        n_offsets = (GQA_BAND - Q_BLOCK) // Q_BLOCK + 1
        scratch.append(pltpu.VMEM((n_offsets, GQA_GROUP * Q_BLOCK, GQA_BAND), F32))
    out = pl.pallas_call(
        functools.partial(_gqa_attn_kernel, seq=seq, windowed=sink is not None),
        grid=(batch, N_KV_GQA),
        in_specs=in_specs,
        out_specs=q_spec,
        out_shape=jax.ShapeDtypeStruct((batch, seq, D_MODEL), BF16),
        scratch_shapes=scratch,
        compiler_params=_params(("arbitrary", "arbitrary")),
        name="gqa_attention",
    )(*args)
    return out.reshape(batch * seq, D_MODEL)


def _rope_tables(seq):
    t = jnp.arange(seq)
    rows = (t // GRID_W).astype(F32)
    cols = (t % GRID_W).astype(F32)
    quarter = HEAD_DIM // 4
    freqs = jnp.exp(-math.log(ROPE_BASE) * jnp.arange(quarter, dtype=F32) / quarter)
    ang_r = rows[:, None] * freqs[None, :]
    ang_c = cols[:, None] * freqs[None, :]
    cos = jnp.concatenate([jnp.cos(ang_r)] * 2 + [jnp.cos(ang_c)] * 2, axis=1)
    sin = jnp.concatenate([-jnp.sin(ang_r), jnp.sin(ang_r), -jnp.sin(ang_c), jnp.sin(ang_c)], axis=1)
    return jnp.tile(cos, (1, LANES // HEAD_DIM)), jnp.tile(sin, (1, LANES // HEAD_DIM))


def kernel(x_prompt, x_sample, cache_k_a, cache_v_a, cache_k_b, cache_v_b, cache_k_c, cache_v_c, c, c_ctx,
           ada_w, ada_b, norm_mix_g, norm_mlp_g, w_o, mlp_w1, mlp_b1, mlp_w2, mlp_b2, w_qkv_a, rpb_a,
           w_qkv_b, sink_b, w_qkv_c, q_norm_c, k_norm_c, final_norm_g):
    batch, seq, _ = x_prompt.shape
    dec_batch, dec_seq, _ = x_sample.shape
    depth = ada_w.shape[0]
    assert dec_batch + 1 <= COND_ROWS and dec_seq % GRID_W == 0
    assert (dec_seq // GRID_W) % NA_ROWS == 0 and dec_seq // GRID_W >= 3 * NA_ROWS

    cond = jnp.concatenate([c, c_ctx[None, :], jnp.zeros((COND_ROWS - dec_batch - 1, D_MODEL), F32)], axis=0)
    mods = _ada_mods(cond, ada_w, ada_b).reshape(depth, COND_ROWS, ADA_CHUNKS, D_MODEL)

    w_qkv = (w_qkv_a, w_qkv_b, w_qkv_c)
    n_kv = (N_KV_A, N_KV_GQA, N_KV_GQA)
    caches_k = (cache_k_a, cache_k_b, cache_k_c)
    caches_v = (cache_v_a, cache_v_b, cache_v_c)
    new_kv = [None] * N_MIXERS
    rope = _rope_tables(dec_seq)
    nq = N_HEADS * HEAD_DIM

    xp = x_prompt.reshape(batch * seq, D_MODEL)
    xs = x_sample.reshape(dec_batch * dec_seq, D_MODEL)

    for l in range(depth):
        m, j = l % N_MIXERS, l // N_MIXERS
        nk = n_kv[m] * HEAD_DIM
        w = w_qkv[m][j].astype(BF16)
        wq, wk, wv = w[:, :nq], w[:, nq:nq + nk], w[:, nq + nk:]
        wo = w_o[l].astype(BF16)
        w1 = mlp_w1[l].astype(BF16)
        w2 = mlp_w2[l].astype(BF16)
        b1 = mlp_b1[l][None, :]
        b2 = mlp_b2[l][None, :]
        g_mix = norm_mix_g[l][None, :]
        g_mlp = norm_mlp_g[l][None, :]
        mods_ctx = mods[l, dec_batch:dec_batch + 1]
        mods_lat = mods[l, :dec_batch]
        final = final_norm_g[None, :] if l == depth - 1 else None
        norm_args, ctx_norm_args = {}, {}
        if m == 2:
            qn = jnp.tile(q_norm_c[j], LANES // HEAD_DIM)[None, :]
            norm_args = dict(qn=qn, kn=jnp.tile(k_norm_c[j], LANES // HEAD_DIM)[None, :])
            ctx_norm_args = dict(qn=qn, kn_col=jnp.tile(k_norm_c[j], n_kv[m])[:, None])
        sink = sink_b[j] if m == 1 else None

        o, kt, vt = _ctx_front(xp, mods_ctx, g_mix, wq, wk.T, wv.T, n_kv=n_kv[m], seq=seq, prev=new_kv[m],
                               sink=sink, **ctx_norm_args)
        xp = _mlp_block(xp, o, mods_ctx, wo, g_mlp, w1, b1, w2, b2, final)
        new_kv[m] = (kt, vt)

        if m == 0:
            q, k, v = _qkv_proj(xs, mods_lat, g_mix, wq, wk, wv, n_kv=n_kv[m], kv_dtype=BF16, dup_kv=False)
            o = _na_attention(q, k, v, caches_k[m], caches_v[m], _na_padded_rpb(rpb_a[j]),
                              batch=dec_batch, seq=dec_seq, layer_slot=j)
        else:
            q, k, v = _qkv_proj(xs, mods_lat, g_mix, wq, wk, wv, n_kv=n_kv[m], kv_dtype=BF16, dup_kv=True,
                                rope=rope, seq=dec_seq, **norm_args)
            o = _gqa_attention(q, k, v, caches_k[m], caches_v[m], batch=dec_batch, seq=dec_seq, layer_slot=j, sink=sink)
        xs = _mlp_block(xs, o, mods_lat, wo, g_mlp, w1, b1, w2, b2, final)

    y_prompt = xp.reshape(batch, seq, D_MODEL)
    y_sample = xs.reshape(dec_batch, dec_seq, D_MODEL)
    outs = [y_prompt, y_sample]
    for m in range(N_MIXERS):
        for stacked in new_kv[m]:
            t = stacked.reshape(batch, stacked.shape[1], n_kv[m], HEAD_DIM, seq)
            outs.append(jnp.transpose(t, (0, 1, 4, 2, 3)))
    return tuple(outs)
```

```python
import functools
import math

import numpy as np
import jax
import jax.numpy as jnp
from jax import lax
from jax.experimental import pallas as pl
from jax.experimental.pallas import tpu as pltpu

D_MODEL = 1024
N_HEADS = 16
HEAD_DIM = 64
N_KV_A = 16
N_KV_GQA = 4
GQA_GROUP = N_HEADS // N_KV_GQA
D_FF = 4 * D_MODEL
ADA_CHUNKS = 6
N_MIXERS = 3
GRID_W = 64
WINDOW = 128
WIN_H = 8
WIN_W = 16
ROPE_BASE = 10000.0
EPS = 1e-6
NEG_INF = -1e30
LOG2E = math.log2(math.e)
Q_SCALE = HEAD_DIM ** -0.5 * LOG2E

HEAD_SHIFT = HEAD_DIM.bit_length() - 1
LANES = 128
COND_ROWS = 16
TOKEN_TILE = 512
MLP_CHUNK = 1024
QKV_TILE = 1024
QKV_SUB_TILE = 512
NA_ROWS = 4
NA_BAND = NA_ROWS + WIN_H
NA_BATCH = 2
Q_BLOCK = 128
GQA_BAND = Q_BLOCK + 2 * WINDOW
VMEM_LIMIT = 56 * 1024 * 1024

BF16 = jnp.bfloat16
F32 = jnp.float32


def _dot(a, b):
    return jnp.dot(a, b, preferred_element_type=F32)


def _dot_nt(a, b):
    return lax.dot_general(a, b, (((1,), (1,)), ((), ())), preferred_element_type=F32)


def _rmsnorm_rows(x, g):
    return x * lax.rsqrt(jnp.mean(x * x, axis=-1, keepdims=True) + EPS) * g


def _lane_is_low(shape=(1, LANES)):
    return lax.broadcasted_iota(jnp.int32, shape, len(shape) - 1) < HEAD_DIM


def _const_spec(shape):
    return pl.BlockSpec(shape, lambda *_: (0,) * len(shape), pipeline_mode=pl.Buffered(1))


def _params(semantics):
    return pltpu.CompilerParams(dimension_semantics=semantics, vmem_limit_bytes=VMEM_LIMIT)


def _ada_kernel(c_ref, w_ref, b_ref, o_ref):
    c = c_ref[...]
    s = (c / (1.0 + jnp.exp(-c))).astype(BF16)
    o_ref[...] = _dot(s, w_ref[...].astype(BF16)) + b_ref[...]


def _ada_mods(cond, ada_w, ada_b):
    depth, _, n_out = ada_w.shape
    nb = n_out // D_MODEL
    return pl.pallas_call(
        _ada_kernel,
        grid=(depth, nb),
        in_specs=[
            pl.BlockSpec((COND_ROWS, D_MODEL), lambda l, n: (0, 0)),
            pl.BlockSpec((None, D_MODEL, D_MODEL), lambda l, n: (l, 0, n)),
            pl.BlockSpec((None, 1, D_MODEL), lambda l, n: (l, 0, n)),
        ],
        out_specs=pl.BlockSpec((None, COND_ROWS, D_MODEL), lambda l, n: (l, 0, n)),
        out_shape=jax.ShapeDtypeStruct((depth, COND_ROWS, n_out), F32),
        compiler_params=_params(("arbitrary", "arbitrary")),
        name="ada_mods",
    )(cond, ada_w, ada_b.reshape(depth, 1, n_out))


def _head_rmsnorm(blk, gain):
    sq = blk * blk
    hi = sq.astype(BF16)
    lo = (sq - hi.astype(F32)).astype(BF16)
    r = lax.shift_right_logical(lax.broadcasted_iota(jnp.int32, (LANES, LANES), 0), HEAD_SHIFT)
    c = lax.shift_right_logical(lax.broadcasted_iota(jnp.int32, (LANES, LANES), 1), HEAD_SHIFT)
    avg = jnp.where(r == c, 1.0 / HEAD_DIM, 0.0).astype(BF16)
    ms = _dot(hi, avg) + _dot(lo, avg)
    return blk * lax.rsqrt(ms + EPS) * gain


def _rope_block(blk, cos, sin_signed):
    lane = lax.broadcasted_iota(jnp.int32, (1, LANES), 1)
    first = (lane & 16) == 0
    partner = jnp.where(first, pltpu.roll(blk, LANES - 16, 1), pltpu.roll(blk, 16, 1))
    return blk * cos + partner * sin_signed


def _qkv_kernel(*refs, n_kv, qk_norm, rope, dup_kv):
    x_ref, mods_ref, g_ref, wq_ref, wk_ref, wv_ref = refs[:6]
    pos = 6
    if qk_norm:
        qn_ref, kn_ref = refs[pos:pos + 2]
        pos += 2
    if rope:
        cos_ref, sin_ref = refs[pos:pos + 2]
        pos += 2
    q_ref, k_ref, v_ref = refs[pos:pos + 3]
    low = _lane_is_low()
    n_sub = x_ref.shape[0] // QKV_SUB_TILE

    def project(i):
        rows = slice(i * QKV_SUB_TILE, (i + 1) * QKV_SUB_TILE)
        h = _rmsnorm_rows(x_ref[rows, :], g_ref[...]) * (1.0 + mods_ref[1:2, :]) + mods_ref[0:1, :]
        h = h.astype(BF16)
        return _dot(h, wq_ref[...]), _dot(h, wk_ref[...]), _dot(h, wv_ref[...])

    def finish(i, qkv):
        q, k, v = qkv
        rows = slice(i * QKV_SUB_TILE, (i + 1) * QKV_SUB_TILE)

        def post(blk, gain_ref):
            if qk_norm:
                blk = _head_rmsnorm(blk, gain_ref[...])
            if rope:
                blk = _rope_block(blk, cos_ref[rows, :], sin_ref[rows, :])
            return blk

        for cb in range(N_HEADS * HEAD_DIM // LANES):
            sl = slice(cb * LANES, (cb + 1) * LANES)
            blk = post(q[:, sl], qn_ref if qk_norm else None)
            q_ref[rows, sl] = (blk * Q_SCALE).astype(q_ref.dtype)

        for cb in range(n_kv * HEAD_DIM // LANES):
            sl = slice(cb * LANES, (cb + 1) * LANES)
            kb = post(k[:, sl], kn_ref if qk_norm else None)
            vb = v[:, sl]
            if dup_kv:
                for arr, ref in ((kb, k_ref), (vb, v_ref)):
                    rolled = pltpu.roll(arr, HEAD_DIM, 1)
                    ref[rows, (2 * cb) * LANES:(2 * cb + 1) * LANES] = jnp.where(low, arr, rolled).astype(ref.dtype)
                    ref[rows, (2 * cb + 1) * LANES:(2 * cb + 2) * LANES] = jnp.where(low, rolled, arr).astype(ref.dtype)
            else:
                k_ref[rows, sl] = kb.astype(k_ref.dtype)
                v_ref[rows, sl] = vb.astype(v_ref.dtype)

    nxt = project(0)
    for i in range(n_sub):
        cur, nxt = nxt, (project(i + 1) if i + 1 < n_sub else None)
        finish(i, cur)


def _ctx_front_kernel(*refs, n_kv, qk_norm, seq, n_prev, has_sink):
    if has_sink:
        sink_ref, refs = refs[0], refs[1:]
    x_ref, mods_ref, g_ref, wq_ref, wkt_ref, wvt_ref = refs[:6]
    pos = 6
    if qk_norm:
        qn_ref, kn_ref = refs[pos:pos + 2]
        pos += 2
    if n_prev:
        kprev_ref, vprev_ref = refs[pos:pos + 2]
    o_ref, kt_ref, vt_ref = refs[-3:]
    low = _lane_is_low()
    n_pairs = N_HEADS // 2
    n_sub = x_ref.shape[0] // seq
    ones = jnp.ones((HEAD_DIM, seq), BF16)

    if n_prev:
        kt_ref[:, :n_prev] = kprev_ref[...]
        vt_ref[:, :n_prev] = vprev_ref[...]

    def head_rows(h):
        kvh = h * n_kv // N_HEADS
        return slice(kvh * HEAD_DIM, (kvh + 1) * HEAD_DIM)

    def project(b):
        rows = slice(b * seq, (b + 1) * seq)
        h = _rmsnorm_rows(x_ref[rows, :], g_ref[...]) * (1.0 + mods_ref[1:2, :]) + mods_ref[0:1, :]
        h = h.astype(BF16)
        q = _dot(h, wq_ref[...])
        q_pairs = []
        for cb in range(n_pairs):
            blk = q[:, cb * LANES:(cb + 1) * LANES]
            if qk_norm:
                blk = _head_rmsnorm(blk, qn_ref[...])
            q_pairs.append((blk * Q_SCALE).astype(BF16))
        kt = _dot_nt(wkt_ref[...], h)
        vt = _dot_nt(wvt_ref[...], h)
        if qk_norm:
            k3 = kt.reshape(n_kv, HEAD_DIM, seq)
            k3 = k3 * lax.rsqrt(jnp.mean(k3 * k3, axis=1, keepdims=True) + EPS)
            kt = k3.reshape(kt.shape) * kn_ref[...]
        kt_ref[b, n_prev] = kt
        vt_ref[b, n_prev] = vt
        return q_pairs, kt.astype(BF16), vt.astype(BF16)

    def attend(b, projected):
        q_pairs, kt, vt = projected
        rows = slice(b * seq, (b + 1) * seq)

        def scores_of(p):
            ktp = jnp.concatenate([kt[head_rows(2 * p), :], kt[head_rows(2 * p + 1), :]], axis=0)
            return [_dot(_stack_heads(q_pairs[p], low), ktp)]

        nxt = scores_of(0)
        for p in range(n_pairs):
            cur, nxt = nxt, (scores_of(p + 1) if p + 1 < n_pairs else None)
            sink_col = None
            if has_sink:
                first_head = lax.broadcasted_iota(jnp.int32, (2 * seq, 1), 0) < seq
                sink_col = jnp.where(first_head, sink_ref[2 * p], sink_ref[2 * p + 1]) * LOG2E
            ps, extra = _softmax_numerators(cur, sink_col)
            v4t = jnp.concatenate([vt[head_rows(2 * p), :], ones, ones, vt[head_rows(2 * p + 1), :]], axis=0)
            even, odd = _stacked_pv(ps, [v4t], low, extra, values_transposed=True)
            o_ref[rows, p * LANES:(p + 1) * LANES] = jnp.where(low, even, odd).astype(o_ref.dtype)

    nxt = project(0)
    for b in range(n_sub):
        cur, nxt = nxt, (project(b + 1) if b + 1 < n_sub else None)
        attend(b, cur)


def _ctx_front(x, mods, gain, wq, wkt, wvt, *, n_kv, seq, qn=None, kn_col=None, prev=None, sink=None):
    tokens = x.shape[0]
    tm = TOKEN_TILE
    steps = tokens // tm
    nk = n_kv * HEAD_DIM
    n_prev = 0 if prev is None else prev[0].shape[1]
    in_specs = [
        pl.BlockSpec((tm, D_MODEL), lambda i: (i, 0)),
        pl.BlockSpec((None, ADA_CHUNKS, D_MODEL), lambda i: (0, 0, 0)),
        _const_spec((1, D_MODEL)),
        _const_spec((D_MODEL, D_MODEL)),
        _const_spec((nk, D_MODEL)),
        _const_spec((nk, D_MODEL)),
    ]
    args = [x, mods, gain, wq, wkt, wvt]
    if qn is not None:
        in_specs += [_const_spec((1, LANES)), _const_spec((nk, 1))]
        args += [qn, kn_col]
    if n_prev:
        in_specs += [pl.BlockSpec((tm // seq, n_prev, nk, seq), lambda i: (i, 0, 0, 0))] * 2
        args += list(prev)
    if sink is not None:
        in_specs.insert(0, pl.BlockSpec(memory_space=pltpu.SMEM))
        args.insert(0, sink)
    kv_spec = pl.BlockSpec((tm // seq, n_prev + 1, nk, seq), lambda i: (i, 0, 0, 0))
    kv_shape = jax.ShapeDtypeStruct((tokens // seq, n_prev + 1, nk, seq), F32)
    return pl.pallas_call(
        functools.partial(_ctx_front_kernel, n_kv=n_kv, qk_norm=qn is not None, seq=seq, n_prev=n_prev,
                          has_sink=sink is not None),
        grid=(steps,),
        in_specs=in_specs,
        out_specs=[pl.BlockSpec((tm, D_MODEL), lambda i: (i, 0)), kv_spec, kv_spec],
        out_shape=[jax.ShapeDtypeStruct((tokens, D_MODEL), BF16), kv_shape, kv_shape],
        compiler_params=_params(("arbitrary",)),
        name="ctx_front",
    )(*args)


def _qkv_proj(x, mods, gain, wq, wk, wv, *, n_kv, kv_dtype, dup_kv, qn=None, kn=None, rope=None, seq=None):
    tokens = x.shape[0]
    tm = QKV_TILE
    steps = tokens // tm
    tiles_per_cond = steps // mods.shape[0]
    nk = n_kv * HEAD_DIM
    nk_out = 2 * nk if dup_kv else nk
    in_specs = [
        pl.BlockSpec((tm, D_MODEL), lambda i: (i, 0)),
        pl.BlockSpec((None, ADA_CHUNKS, D_MODEL), lambda i: (i // tiles_per_cond, 0, 0)),
        _const_spec((1, D_MODEL)),
        _const_spec((D_MODEL, D_MODEL)),
        _const_spec((D_MODEL, nk)),
        _const_spec((D_MODEL, nk)),
    ]
    args = [x, mods, gain, wq, wk, wv]
    if qn is not None:
        in_specs += [_const_spec((1, LANES)), _const_spec((1, LANES))]
        args += [qn, kn]
    if rope is not None:
        tiles_per_seq = seq // tm
        in_specs += [pl.BlockSpec((tm, LANES), lambda i: (i % tiles_per_seq, 0))] * 2
        args += list(rope)
    return pl.pallas_call(
        functools.partial(_qkv_kernel, n_kv=n_kv, qk_norm=qn is not None, rope=rope is not None, dup_kv=dup_kv),
        grid=(steps,),
        in_specs=in_specs,
        out_specs=[
            pl.BlockSpec((tm, D_MODEL), lambda i: (i, 0)),
            pl.BlockSpec((tm, nk_out), lambda i: (i, 0)),
            pl.BlockSpec((tm, nk_out), lambda i: (i, 0)),
        ],
        out_shape=[
            jax.ShapeDtypeStruct((tokens, D_MODEL), BF16),
            jax.ShapeDtypeStruct((tokens, nk_out), kv_dtype),
            jax.ShapeDtypeStruct((tokens, nk_out), kv_dtype),
        ],
        compiler_params=_params(("arbitrary",)),
        name="qkv_proj",
    )(*args)


def _mlp_kernel(*refs, final, n_cast):
    x_ref, o_ref, mods_ref, wo_ref, g_ref, w1_ref, b1_ref, w2_ref, b2_ref = refs[:9]
    pos = 9
    fg_ref = refs[pos] if final else None
    pos += int(final)
    cast_in = refs[pos:pos + n_cast]
    out_ref = refs[pos + n_cast]
    cast_out = refs[pos + n_cast + 1:]

    x1 = x_ref[...] + mods_ref[2:3, :] * _dot(o_ref[...], wo_ref[...])
    h = _rmsnorm_rows(x1, g_ref[...]) * (1.0 + mods_ref[4:5, :]) + mods_ref[3:4, :]
    h = h.astype(BF16)
    acc = jnp.zeros_like(x1)
    for c in range(D_FF // MLP_CHUNK):
        sl = slice(c * MLP_CHUNK, (c + 1) * MLP_CHUNK)
        t = jnp.maximum(_dot(h, w1_ref[:, sl]) + b1_ref[:, sl], 0.0)
        acc = acc + _dot((t * t).astype(BF16), w2_ref[sl, :])
    x2 = x1 + mods_ref[5:6, :] * (acc + b2_ref[...])
    if final:
        x2 = _rmsnorm_rows(x2, fg_ref[...])
    out_ref[...] = x2
    for src, dst in zip(cast_in, cast_out):
        dst[...] = src[...].astype(dst.dtype)


def _mlp_block(x, o, mods, wo, gain, w1, b1, w2, b2, final_gain=None, to_cast=()):
    tokens = x.shape[0]
    tm = TOKEN_TILE
    steps = tokens // tm
    tiles_per_cond = steps // mods.shape[0]
    in_specs = [
        pl.BlockSpec((tm, D_MODEL), lambda i: (i, 0)),
        pl.BlockSpec((tm, D_MODEL), lambda i: (i, 0)),
        pl.BlockSpec((None, ADA_CHUNKS, D_MODEL), lambda i: (i // tiles_per_cond, 0, 0)),
        _const_spec((D_MODEL, D_MODEL)),
        _const_spec((1, D_MODEL)),
        _const_spec((D_MODEL, D_FF)),
        _const_spec((1, D_FF)),
        _const_spec((D_FF, D_MODEL)),
        _const_spec((1, D_MODEL)),
    ]
    args = [x, o, mods, wo, gain, w1, b1, w2, b2]
    if final_gain is not None:
        in_specs.append(_const_spec((1, D_MODEL)))
        args.append(final_gain)
    out_specs = [pl.BlockSpec((tm, D_MODEL), lambda i: (i, 0))]
    out_shape = [jax.ShapeDtypeStruct((tokens, D_MODEL), F32)]
    for w, layer, col0, ncols in to_cast:
        n_layers, rows, cols = w.shape
        slab = rows // steps
        assert rows % steps == 0 and slab % 16 == 0 and col0 % ncols == 0
        in_specs.append(pl.BlockSpec((None, None, slab, ncols), lambda i, la=layer, cb=col0 // ncols: (la, i, 0, cb)))
        args.append(w.reshape(n_layers, steps, slab, cols))
        out_specs.append(pl.BlockSpec((None, slab, ncols), lambda i: (i, 0, 0)))
        out_shape.append(jax.ShapeDtypeStruct((steps, slab, ncols), BF16))
    outs = pl.pallas_call(
        functools.partial(_mlp_kernel, final=final_gain is not None, n_cast=len(to_cast)),
        grid=(steps,),
        in_specs=in_specs,
        out_specs=out_specs,
        out_shape=out_shape,
        compiler_params=_params(("arbitrary",)),
        name="mlp_block",
    )(*args)
    casted = [c.reshape(w.shape[1], ncols) for c, (w, _, _, ncols) in zip(outs[1:], to_cast)]
    return outs[0], casted


def _softmax_numerators(scores, extra_logit=None):
    m = functools.reduce(jnp.maximum, [jnp.max(s, axis=-1, keepdims=True) for s in scores])
    if extra_logit is not None:
        m = jnp.maximum(m, extra_logit)
    ps = [jnp.exp2(s - m).astype(BF16) for s in scores]
    return ps, (None if extra_logit is None else jnp.exp2(extra_logit - m))


def _with_ones(vals, value_lanes):
    return jnp.where(value_lanes, vals, jnp.ones_like(vals))


def _dup_head(blk, keep):
    return jnp.where(keep, blk, pltpu.roll(blk, HEAD_DIM, 1))


def _stack_heads(qb, low):
    zero = jnp.zeros_like(qb)
    return jnp.concatenate([jnp.where(low, qb, zero), jnp.where(jnp.logical_not(low), qb, zero)], axis=0)


def _values_with_ones(vb, low):
    return jnp.concatenate([_with_ones(vb, low), _with_ones(vb, jnp.logical_not(low))], axis=1)


def _stacked_pv(ps, v4s, low, extra=None, values_transposed=False):
    high = jnp.logical_not(low)
    mm = _dot_nt if values_transposed else _dot
    acc = functools.reduce(jnp.add, [mm(p, v) for p, v in zip(ps, v4s)])
    half = acc.shape[0] // 2
    outs = []
    for rows, lanes, ones_lanes in ((slice(0, half), slice(0, LANES), high), (slice(half, None), slice(LANES, None), low)):
        part = acc[rows, lanes]
        if extra is not None:
            part = part + jnp.where(ones_lanes, extra[rows], 0.0)
        denom = jnp.max(jnp.where(ones_lanes, part, 0.0), axis=-1, keepdims=True)
        outs.append(part * (1.0 / denom))
    return outs


def _na_group_geometry(rows):
    kh = min(WIN_H, rows)
    n_groups = rows // NA_ROWS
    masked = 2 * WIN_H - 1
    band0 = [int(np.clip(g * NA_ROWS - kh // 2, 0, rows - NA_BAND)) for g in range(n_groups)]
    table_of = [0 if g == 0 else (2 if g == n_groups - 1 else 1) for g in range(n_groups)]
    sel = np.full((3, NA_ROWS, NA_BAND), masked, np.int64)
    for g in (0, 1, n_groups - 1):
        for a in range(NA_ROWS):
            r = g * NA_ROWS + a
            r0 = int(np.clip(r - kh // 2, 0, rows - kh))
            for i in range(NA_BAND):
                kr = band0[g] + i
                if r0 <= kr < r0 + kh:
                    sel[table_of[g], a, i] = kr - r + WIN_H - 1
    return band0, table_of, sel


def _na_attn_kernel(q_ref, k_ref, v_ref, kc_ref, vc_ref, rpb_ref, o_ref, tab_scr, v4_scr, *, rows):
    low = _lane_is_low()
    band0, table_of, sel = _na_group_geometry(rows)
    q_rows = NA_ROWS * GRID_W
    k_rows = NA_BAND * GRID_W
    n_groups = rows // NA_ROWS

    @pl.when(pl.program_id(1) == 0)
    def _():
        qc = lax.broadcasted_iota(jnp.int32, (GRID_W, GRID_W), 0)
        kc = lax.broadcasted_iota(jnp.int32, (GRID_W, GRID_W), 1)
        c0 = jnp.clip(qc - WIN_W // 2, 0, GRID_W - WIN_W)
        col_valid = (kc >= c0) & (kc < c0 + WIN_W)
        for half in range(2):
            blocks = []
            for ri in range(2 * WIN_H - 1):
                row = jnp.broadcast_to(rpb_ref[half, ri:ri + 1, :], (GRID_W, 2 * GRID_W))
                skew = pltpu.roll(row, GRID_W + 1, 1, stride=1, stride_axis=0)
                blocks.append(jnp.where(col_valid, skew[:, :GRID_W], NEG_INF))
            blocks.append(jnp.full((GRID_W, GRID_W), NEG_INF, F32))
            for t in range(3):
                for a in range(NA_ROWS):
                    strip = [blocks[int(sel[t, a, i])] for i in range(NA_BAND)]
                    r = half * q_rows + a * GRID_W
                    tab_scr[t, r:r + GRID_W, :] = jnp.concatenate(strip, axis=1)

    n_batch = q_ref.shape[0]
    kcb = [kc_ref[bb].astype(BF16) for bb in range(n_batch)]
    vc4 = [_values_with_ones(vc_ref[bb].astype(BF16), low) for bb in range(n_batch)]
    for bb in range(n_batch):
        v4_scr[bb] = _values_with_ones(v_ref[bb], low)

    def scores_of(item):
        bb, g = item
        rq = g * q_rows
        ks = band0[g] * GRID_W
        qs = _stack_heads(q_ref[bb, rq:rq + q_rows, :], low)
        return [_dot_nt(qs, k_ref[bb, ks:ks + k_rows, :]) + tab_scr[table_of[g]], _dot_nt(qs, kcb[bb])]

    items = [(bb, g) for bb in range(n_batch) for g in range(n_groups)]
    nxt = scores_of(items[0])
    for idx, (bb, g) in enumerate(items):
        cur, nxt = nxt, (scores_of(items[idx + 1]) if idx + 1 < len(items) else None)
        ks = band0[g] * GRID_W
        ps, _ = _softmax_numerators(cur)
        lo, hi = _stacked_pv(ps, [v4_scr[bb, ks:ks + k_rows, :], vc4[bb]], low)
        o_ref[bb, g * q_rows:(g + 1) * q_rows, :] = jnp.where(low, lo, hi).astype(o_ref.dtype)


def _na_padded_rpb(rpb):
    pad_l = GRID_W - WIN_W
    return jnp.pad(rpb * LOG2E, ((0, 0), (0, 1), (pad_l, 2 * GRID_W - pad_l - rpb.shape[2])))


def _na_attention(q, k, v, cache_k, cache_v, rpb_pad, *, batch, seq, layer_slot):
    rows = seq // GRID_W
    n_pairs = N_HEADS // 2
    q3, k3, v3 = (a.reshape(batch, seq, D_MODEL) for a in (q, k, v))
    past = cache_k.shape[2]
    ck = cache_k.reshape(batch, cache_k.shape[1], past, D_MODEL)
    cv = cache_v.reshape(batch, cache_v.shape[1], past, D_MODEL)
    nb = NA_BATCH
    tok_spec = pl.BlockSpec((nb, seq, LANES), lambda p, b: (b, 0, p))
    ctx_spec = pl.BlockSpec((nb, None, past, LANES), lambda p, b: (b, layer_slot, 0, p))
    out = pl.pallas_call(
        functools.partial(_na_attn_kernel, rows=rows),
        grid=(n_pairs, batch // nb),
        in_specs=[tok_spec, tok_spec, tok_spec, ctx_spec, ctx_spec,
                  pl.BlockSpec((2,) + rpb_pad.shape[1:], lambda p, b: (p, 0, 0))],
        out_specs=tok_spec,
        out_shape=jax.ShapeDtypeStruct((batch, seq, D_MODEL), BF16),
        scratch_shapes=[pltpu.VMEM((3, 2 * NA_ROWS * GRID_W, NA_BAND * GRID_W), F32),
                        pltpu.VMEM((nb, seq, 2 * LANES), BF16)],
        compiler_params=_params(("arbitrary", "arbitrary")),
        name="na_attention",
    )(q3, k3, v3, ck, cv, rpb_pad)
    return out.reshape(batch * seq, D_MODEL)


def _gqa_attn_kernel(*refs, seq, windowed):
    if windowed:
        sink_ref, q_ref, k_ref, v_ref, kc_ref, vc_ref, o_ref, v4_scr, bias_scr = refs
    else:
        q_ref, k_ref, v_ref, kc_ref, vc_ref, o_ref, v4_scr = refs
    kvh = pl.program_id(1)
    lane_half = lax.shift_right_logical(lax.broadcasted_iota(jnp.int32, (1, LANES), 1), HEAD_SHIFT)
    keep = lane_half == (kvh & 1)
    low = lane_half == 0
    high = jnp.logical_not(low)
    band = GQA_BAND
    stacked = GQA_GROUP * Q_BLOCK
    n_blocks = seq // Q_BLOCK

    head_order = list(range(0, GQA_GROUP, 2)) + list(range(1, GQA_GROUP, 2))

    kcb = _dup_head(kc_ref[...], keep).astype(BF16)
    vc4 = _values_with_ones(_dup_head(vc_ref[...], keep).astype(BF16), low)
    v4_scr[...] = _values_with_ones(v_ref[...], low)

    if windowed:
        q_shift = Q_BLOCK.bit_length() - 1
        block_of_row = lax.shift_right_logical(lax.broadcasted_iota(jnp.int32, (stacked, 1), 0), q_shift)
        sink_col = jnp.zeros((stacked, 1), F32)
        for blk_i, g in enumerate(head_order):
            sink_col = jnp.where(block_of_row == blk_i, sink_ref[kvh * GQA_GROUP + g] * LOG2E, sink_col)

        @pl.when((pl.program_id(0) == 0) & (kvh == 0))
        def _():
            q_off = lax.broadcasted_iota(jnp.int32, (stacked, band), 0) & (Q_BLOCK - 1)
            k_off = lax.broadcasted_iota(jnp.int32, (stacked, band), 1)
            for t in range(bias_scr.shape[0]):
                dist = q_off - k_off + t * Q_BLOCK
                bias_scr[t] = jnp.where(jnp.abs(dist) <= WINDOW, 0.0, NEG_INF)

    def key_rows(i):
        if not windowed:
            return slice(None)
        ks = int(np.clip(i * Q_BLOCK - WINDOW, 0, seq - band))
        return slice(ks, ks + band)

    def scores_of(i):
        r0 = i * Q_BLOCK
        qblk = q_ref[r0:r0 + Q_BLOCK, :]
        zero = jnp.zeros((Q_BLOCK, LANES), qblk.dtype)
        parts = []
        for g in head_order:
            blk = qblk[:, (g // 2) * LANES:(g // 2 + 1) * LANES]
            parts.append(jnp.where(low if g % 2 == 0 else high, blk, zero))
        qs = jnp.concatenate(parts, axis=0)
        keys = key_rows(i)
        s_lat = _dot_nt(qs, k_ref[keys, :])
        if windowed:
            s_lat = s_lat + bias_scr[(r0 - keys.start) // Q_BLOCK]
        return [s_lat, _dot_nt(qs, kcb)]

    nxt = scores_of(0)
    for i in range(n_blocks):
        cur, nxt = nxt, (scores_of(i + 1) if i + 1 < n_blocks else None)
        ps, extra = _softmax_numerators(cur, sink_col if windowed else None)
        even, odd = _stacked_pv(ps, [v4_scr[key_rows(i), :], vc4], low, extra)
        r0 = i * Q_BLOCK
        for c in range(GQA_GROUP // 2):
            rows_c = slice(c * Q_BLOCK, (c + 1) * Q_BLOCK)
            o_ref[r0:r0 + Q_BLOCK, c * LANES:(c + 1) * LANES] = jnp.where(low, even[rows_c], odd[rows_c]).astype(o_ref.dtype)


def _gqa_attention(q, kd, vd, cache_k, cache_v, *, batch, seq, layer_slot, sink=None):
    group_w = GQA_GROUP * HEAD_DIM
    q3 = q.reshape(batch, seq, D_MODEL)
    kd3 = kd.reshape(batch, seq, N_KV_GQA * LANES)
    vd3 = vd.reshape(batch, seq, N_KV_GQA * LANES)
    past = cache_k.shape[2]
    ck = cache_k.reshape(batch, cache_k.shape[1], past, N_KV_GQA * HEAD_DIM)
    cv = cache_v.reshape(batch, cache_v.shape[1], past, N_KV_GQA * HEAD_DIM)
    q_spec = pl.BlockSpec((None, seq, group_w), lambda b, h: (b, 0, h))
    kv_spec = pl.BlockSpec((None, seq, LANES), lambda b, h: (b, 0, h))
    ctx_spec = pl.BlockSpec((None, None, past, LANES), lambda b, h: (b, layer_slot, 0, h // 2))
    in_specs = [q_spec, kv_spec, kv_spec, ctx_spec, ctx_spec]
    args = [q3, kd3, vd3, ck, cv]
    scratch = [pltpu.VMEM((seq, 2 * LANES), BF16)]
    if sink is not None:
        in_specs.insert(0, pl.BlockSpec(memory_space=pltpu.SMEM))
        args.insert(0, sink)
        n_offsets = (GQA_BAND - Q_BLOCK) // Q_BLOCK + 1
        scratch.append(pltpu.VMEM((n_offsets, GQA_GROUP * Q_BLOCK, GQA_BAND), F32))
    out = pl.pallas_call(
        functools.partial(_gqa_attn_kernel, seq=seq, windowed=sink is not None),
        grid=(batch, N_KV_GQA),
        in_specs=in_specs,
        out_specs=q_spec,
        out_shape=jax.ShapeDtypeStruct((batch, seq, D_MODEL), BF16),
        scratch_shapes=scratch,
        compiler_params=_params(("arbitrary", "arbitrary")),
        name="gqa_attention",
    )(*args)
    return out.reshape(batch * seq, D_MODEL)


def _rope_tables(seq):
    t = jnp.arange(seq)
    rows = (t // GRID_W).astype(F32)
    cols = (t % GRID_W).astype(F32)
    quarter = HEAD_DIM // 4
    freqs = jnp.exp(-math.log(ROPE_BASE) * jnp.arange(quarter, dtype=F32) / quarter)
    ang_r = rows[:, None] * freqs[None, :]
    ang_c = cols[:, None] * freqs[None, :]
    cos = jnp.concatenate([jnp.cos(ang_r)] * 2 + [jnp.cos(ang_c)] * 2, axis=1)
    sin = jnp.concatenate([-jnp.sin(ang_r), jnp.sin(ang_r), -jnp.sin(ang_c), jnp.sin(ang_c)], axis=1)
    return jnp.tile(cos, (1, LANES // HEAD_DIM)), jnp.tile(sin, (1, LANES // HEAD_DIM))


def kernel(x_prompt, x_sample, cache_k_a, cache_v_a, cache_k_b, cache_v_b, cache_k_c, cache_v_c, c, c_ctx,
           ada_w, ada_b, norm_mix_g, norm_mlp_g, w_o, mlp_w1, mlp_b1, mlp_w2, mlp_b2, w_qkv_a, rpb_a,
           w_qkv_b, sink_b, w_qkv_c, q_norm_c, k_norm_c, final_norm_g):
    batch, seq, _ = x_prompt.shape
    dec_batch, dec_seq, _ = x_sample.shape
    depth = ada_w.shape[0]
    assert dec_batch + 1 <= COND_ROWS and dec_seq % GRID_W == 0
    assert (dec_seq // GRID_W) % NA_ROWS == 0 and dec_seq // GRID_W >= 3 * NA_ROWS

    cond = jnp.concatenate([c, c_ctx[None, :], jnp.zeros((COND_ROWS - dec_batch - 1, D_MODEL), F32)], axis=0)
    mods = _ada_mods(cond, ada_w, ada_b).reshape(depth, COND_ROWS, ADA_CHUNKS, D_MODEL)

    w_qkv = (w_qkv_a, w_qkv_b, w_qkv_c)
    n_kv = (N_KV_A, N_KV_GQA, N_KV_GQA)
    caches_k = (cache_k_a, cache_k_b, cache_k_c)
    caches_v = (cache_v_a, cache_v_b, cache_v_c)
    new_kv = [None] * N_MIXERS
    rope = _rope_tables(dec_seq)
    nq = N_HEADS * HEAD_DIM

    xp = x_prompt.reshape(batch * seq, D_MODEL)
    xs = x_sample.reshape(dec_batch * dec_seq, D_MODEL)

    def layer_weights_f32(l):
        m, j = l % N_MIXERS, l // N_MIXERS
        nk = n_kv[m] * HEAD_DIM
        return [(w_qkv[m], j, 0, nq), (w_qkv[m], j, nq, nk), (w_qkv[m], j, nq + nk, nk), (w_o, l, 0, D_MODEL),
                (mlp_w1, l, 0, D_FF), (mlp_w2, l, 0, D_MODEL)]

    weights = [w[la, :, c0:c0 + nc].astype(BF16) for w, la, c0, nc in layer_weights_f32(0)]

    for l in range(depth):
        m, j = l % N_MIXERS, l // N_MIXERS
        wq, wk, wv, wo, w1, w2 = weights
        b1 = mlp_b1[l][None, :]
        b2 = mlp_b2[l][None, :]
        g_mix = norm_mix_g[l][None, :]
        g_mlp = norm_mlp_g[l][None, :]
        mods_ctx = mods[l, dec_batch:dec_batch + 1]
        mods_lat = mods[l, :dec_batch]
        final = final_norm_g[None, :] if l == depth - 1 else None
        norm_args, ctx_norm_args = {}, {}
        if m == 2:
            qn = jnp.tile(q_norm_c[j], LANES // HEAD_DIM)[None, :]
            norm_args = dict(qn=qn, kn=jnp.tile(k_norm_c[j], LANES // HEAD_DIM)[None, :])
            ctx_norm_args = dict(qn=qn, kn_col=jnp.tile(k_norm_c[j], n_kv[m])[:, None])
        sink = sink_b[j] if m == 1 else None

        o, kt, vt = _ctx_front(xp, mods_ctx, g_mix, wq, wk.T, wv.T, n_kv=n_kv[m], seq=seq, prev=new_kv[m],
                               sink=sink, **ctx_norm_args)
        xp, _ = _mlp_block(xp, o, mods_ctx, wo, g_mlp, w1, b1, w2, b2, final)
        new_kv[m] = (kt, vt)

        if m == 0:
            q, k, v = _qkv_proj(xs, mods_lat, g_mix, wq, wk, wv, n_kv=n_kv[m], kv_dtype=BF16, dup_kv=False)
            o = _na_attention(q, k, v, caches_k[m], caches_v[m], _na_padded_rpb(rpb_a[j]),
                              batch=dec_batch, seq=dec_seq, layer_slot=j)
        else:
            q, k, v = _qkv_proj(xs, mods_lat, g_mix, wq, wk, wv, n_kv=n_kv[m], kv_dtype=BF16, dup_kv=True,
                                rope=rope, seq=dec_seq, **norm_args)
            o = _gqa_attention(q, k, v, caches_k[m], caches_v[m], batch=dec_batch, seq=dec_seq, layer_slot=j, sink=sink)
        to_cast = layer_weights_f32(l + 1) if l + 1 < depth else ()
        xs, weights = _mlp_block(xs, o, mods_lat, wo, g_mlp, w1, b1, w2, b2, final, to_cast=to_cast)

    y_prompt = xp.reshape(batch, seq, D_MODEL)
    y_sample = xs.reshape(dec_batch, dec_seq, D_MODEL)
    outs = [y_prompt, y_sample]
    for m in range(N_MIXERS):
        for stacked in new_kv[m]:
            t = stacked.reshape(batch, stacked.shape[1], n_kv[m], HEAD_DIM, seq)
            outs.append(jnp.transpose(t, (0, 1, 4, 2, 3)))
    return tuple(outs)
```

```python
import functools
import math

import numpy as np
import jax
import jax.numpy as jnp
from jax import lax
from jax.experimental import pallas as pl
from jax.experimental.pallas import tpu as pltpu

D_MODEL = 1024
N_HEADS = 16
HEAD_DIM = 64
N_KV_A = 16
N_KV_GQA = 4
GQA_GROUP = N_HEADS // N_KV_GQA
D_FF = 4 * D_MODEL
ADA_CHUNKS = 6
N_MIXERS = 3
GRID_W = 64
WINDOW = 128
WIN_H = 8
WIN_W = 16
ROPE_BASE = 10000.0
EPS = 1e-6
NEG_INF = -1e30
LOG2E = math.log2(math.e)
Q_SCALE = HEAD_DIM ** -0.5 * LOG2E

HEAD_SHIFT = HEAD_DIM.bit_length() - 1
LANES = 128
COND_ROWS = 16
TOKEN_TILE = 512
MLP_CHUNK = 1024
QKV_TILE = 1024
QKV_SUB_TILE = 512
NA_ROWS = 4
NA_BAND = NA_ROWS + WIN_H
NA_BATCH = 2
Q_BLOCK = 128
GQA_BAND = Q_BLOCK + 2 * WINDOW
VMEM_LIMIT = 56 * 1024 * 1024

BF16 = jnp.bfloat16
F32 = jnp.float32


def _dot(a, b):
    return jnp.dot(a, b, preferred_element_type=F32)


def _dot_nt(a, b):
    return lax.dot_general(a, b, (((1,), (1,)), ((), ())), preferred_element_type=F32)


def _rmsnorm_rows(x, g):
    return x * lax.rsqrt(jnp.mean(x * x, axis=-1, keepdims=True) + EPS) * g


def _lane_is_low(shape=(1, LANES)):
    return lax.broadcasted_iota(jnp.int32, shape, len(shape) - 1) < HEAD_DIM


def _const_spec(shape):
    return pl.BlockSpec(shape, lambda *_: (0,) * len(shape), pipeline_mode=pl.Buffered(1))


def _params(semantics):
    return pltpu.CompilerParams(dimension_semantics=semantics, vmem_limit_bytes=VMEM_LIMIT)


def _ada_kernel(c_ref, w_ref, b_ref, o_ref):
    c = c_ref[...]
    s = (c / (1.0 + jnp.exp(-c))).astype(BF16)
    o_ref[...] = _dot(s, w_ref[...].astype(BF16)) + b_ref[...]


def _ada_mods(cond, ada_w, ada_b):
    depth, _, n_out = ada_w.shape
    nb = n_out // D_MODEL
    return pl.pallas_call(
        _ada_kernel,
        grid=(depth, nb),
        in_specs=[
            pl.BlockSpec((COND_ROWS, D_MODEL), lambda l, n: (0, 0)),
            pl.BlockSpec((None, D_MODEL, D_MODEL), lambda l, n: (l, 0, n)),
            pl.BlockSpec((None, 1, D_MODEL), lambda l, n: (l, 0, n)),
        ],
        out_specs=pl.BlockSpec((None, COND_ROWS, D_MODEL), lambda l, n: (l, 0, n)),
        out_shape=jax.ShapeDtypeStruct((depth, COND_ROWS, n_out), F32),
        compiler_params=_params(("arbitrary", "arbitrary")),
        name="ada_mods",
    )(cond, ada_w, ada_b.reshape(depth, 1, n_out))


def _head_rmsnorm(blk, gain):
    sq = blk * blk
    hi = sq.astype(BF16)
    lo = (sq - hi.astype(F32)).astype(BF16)
    r = lax.shift_right_logical(lax.broadcasted_iota(jnp.int32, (LANES, LANES), 0), HEAD_SHIFT)
    c = lax.shift_right_logical(lax.broadcasted_iota(jnp.int32, (LANES, LANES), 1), HEAD_SHIFT)
    avg = jnp.where(r == c, 1.0 / HEAD_DIM, 0.0).astype(BF16)
    ms = _dot(hi, avg) + _dot(lo, avg)
    return blk * lax.rsqrt(ms + EPS) * gain


def _rope_block(blk, cos, sin_signed):
    lane = lax.broadcasted_iota(jnp.int32, (1, LANES), 1)
    first = (lane & 16) == 0
    partner = jnp.where(first, pltpu.roll(blk, LANES - 16, 1), pltpu.roll(blk, 16, 1))
    return blk * cos + partner * sin_signed


def _qkv_kernel(*refs, n_kv, qk_norm, rope, dup_kv):
    x_ref, mods_ref, g_ref, wq_ref, wk_ref, wv_ref = refs[:6]
    pos = 6
    if qk_norm:
        qn_ref, kn_ref = refs[pos:pos + 2]
        pos += 2
    if rope:
        cos_ref, sin_ref = refs[pos:pos + 2]
        pos += 2
    q_ref, k_ref, v_ref = refs[pos:pos + 3]
    low = _lane_is_low()
    n_sub = x_ref.shape[0] // QKV_SUB_TILE

    def project(i):
        rows = slice(i * QKV_SUB_TILE, (i + 1) * QKV_SUB_TILE)
        h = _rmsnorm_rows(x_ref[rows, :], g_ref[...]) * (1.0 + mods_ref[1:2, :]) + mods_ref[0:1, :]
        h = h.astype(BF16)
        return _dot(h, wq_ref[...]), _dot(h, wk_ref[...]), _dot(h, wv_ref[...])

    def finish(i, qkv):
        q, k, v = qkv
        rows = slice(i * QKV_SUB_TILE, (i + 1) * QKV_SUB_TILE)

        def post(blk, gain_ref):
            if qk_norm:
                blk = _head_rmsnorm(blk, gain_ref[...])
            if rope:
                blk = _rope_block(blk, cos_ref[rows, :], sin_ref[rows, :])
            return blk

        for cb in range(N_HEADS * HEAD_DIM // LANES):
            sl = slice(cb * LANES, (cb + 1) * LANES)
            blk = post(q[:, sl], qn_ref if qk_norm else None)
            q_ref[rows, sl] = (blk * Q_SCALE).astype(q_ref.dtype)

        for cb in range(n_kv * HEAD_DIM // LANES):
            sl = slice(cb * LANES, (cb + 1) * LANES)
            kb = post(k[:, sl], kn_ref if qk_norm else None)
            vb = v[:, sl]
            if dup_kv:
                for arr, ref in ((kb, k_ref), (vb, v_ref)):
                    rolled = pltpu.roll(arr, HEAD_DIM, 1)
                    ref[rows, (2 * cb) * LANES:(2 * cb + 1) * LANES] = jnp.where(low, arr, rolled).astype(ref.dtype)
                    ref[rows, (2 * cb + 1) * LANES:(2 * cb + 2) * LANES] = jnp.where(low, rolled, arr).astype(ref.dtype)
            else:
                k_ref[rows, sl] = kb.astype(k_ref.dtype)
                v_ref[rows, sl] = vb.astype(v_ref.dtype)

    nxt = project(0)
    for i in range(n_sub):
        cur, nxt = nxt, (project(i + 1) if i + 1 < n_sub else None)
        finish(i, cur)


def _ctx_front_kernel(*refs, n_kv, qk_norm, seq, n_prev, has_sink):
    if has_sink:
        sink_ref, refs = refs[0], refs[1:]
    x_ref, mods_ref, g_ref, wq_ref, wkt_ref, wvt_ref = refs[:6]
    pos = 6
    if qk_norm:
        qn_ref, kn_ref = refs[pos:pos + 2]
        pos += 2
    if n_prev:
        kprev_ref, vprev_ref = refs[pos:pos + 2]
    o_ref, kt_ref, vt_ref = refs[-3:]
    low = _lane_is_low()
    n_pairs = N_HEADS // 2
    n_sub = x_ref.shape[0] // seq
    ones = jnp.ones((HEAD_DIM, seq), BF16)

    if n_prev:
        kt_ref[:, :n_prev] = kprev_ref[...]
        vt_ref[:, :n_prev] = vprev_ref[...]

    def head_rows(h):
        kvh = h * n_kv // N_HEADS
        return slice(kvh * HEAD_DIM, (kvh + 1) * HEAD_DIM)

    def project(b):
        rows = slice(b * seq, (b + 1) * seq)
        h = _rmsnorm_rows(x_ref[rows, :], g_ref[...]) * (1.0 + mods_ref[1:2, :]) + mods_ref[0:1, :]
        h = h.astype(BF16)
        q = _dot(h, wq_ref[...])
        q_pairs = []
        for cb in range(n_pairs):
            blk = q[:, cb * LANES:(cb + 1) * LANES]
            if qk_norm:
                blk = _head_rmsnorm(blk, qn_ref[...])
            q_pairs.append((blk * Q_SCALE).astype(BF16))
        kt = _dot_nt(wkt_ref[...], h)
        vt = _dot_nt(wvt_ref[...], h)
        if qk_norm:
            k3 = kt.reshape(n_kv, HEAD_DIM, seq)
            k3 = k3 * lax.rsqrt(jnp.mean(k3 * k3, axis=1, keepdims=True) + EPS)
            kt = k3.reshape(kt.shape) * kn_ref[...]
        kt_ref[b, n_prev] = kt
        vt_ref[b, n_prev] = vt
        return q_pairs, kt.astype(BF16), vt.astype(BF16)

    def attend(b, projected):
        q_pairs, kt, vt = projected
        rows = slice(b * seq, (b + 1) * seq)

        def scores_of(p):
            ktp = jnp.concatenate([kt[head_rows(2 * p), :], kt[head_rows(2 * p + 1), :]], axis=0)
            return [_dot(_stack_heads(q_pairs[p], low), ktp)]

        nxt = scores_of(0)
        for p in range(n_pairs):
            cur, nxt = nxt, (scores_of(p + 1) if p + 1 < n_pairs else None)
            sink_col = None
            if has_sink:
                first_head = lax.broadcasted_iota(jnp.int32, (2 * seq, 1), 0) < seq
                sink_col = jnp.where(first_head, sink_ref[2 * p], sink_ref[2 * p + 1]) * LOG2E
            ps, extra = _softmax_numerators(cur, sink_col)
            v4t = jnp.concatenate([vt[head_rows(2 * p), :], ones, ones, vt[head_rows(2 * p + 1), :]], axis=0)
            even, odd = _stacked_pv(ps, [v4t], low, extra, transposed=(True,))
            o_ref[rows, p * LANES:(p + 1) * LANES] = jnp.where(low, even, odd).astype(o_ref.dtype)

    nxt = project(0)
    for b in range(n_sub):
        cur, nxt = nxt, (project(b + 1) if b + 1 < n_sub else None)
        attend(b, cur)


def _ctx_front(x, mods, gain, wq, wkt, wvt, *, n_kv, seq, qn=None, kn_col=None, prev=None, sink=None):
    tokens = x.shape[0]
    tm = TOKEN_TILE
    steps = tokens // tm
    nk = n_kv * HEAD_DIM
    n_prev = 0 if prev is None else prev[0].shape[1]
    in_specs = [
        pl.BlockSpec((tm, D_MODEL), lambda i: (i, 0)),
        pl.BlockSpec((None, ADA_CHUNKS, D_MODEL), lambda i: (0, 0, 0)),
        _const_spec((1, D_MODEL)),
        _const_spec((D_MODEL, D_MODEL)),
        _const_spec((nk, D_MODEL)),
        _const_spec((nk, D_MODEL)),
    ]
    args = [x, mods, gain, wq, wkt, wvt]
    if qn is not None:
        in_specs += [_const_spec((1, LANES)), _const_spec((nk, 1))]
        args += [qn, kn_col]
    if n_prev:
        in_specs += [pl.BlockSpec((tm // seq, n_prev, nk, seq), lambda i: (i, 0, 0, 0))] * 2
        args += list(prev)
    if sink is not None:
        in_specs.insert(0, pl.BlockSpec(memory_space=pltpu.SMEM))
        args.insert(0, sink)
    kv_spec = pl.BlockSpec((tm // seq, n_prev + 1, nk, seq), lambda i: (i, 0, 0, 0))
    kv_shape = jax.ShapeDtypeStruct((tokens // seq, n_prev + 1, nk, seq), F32)
    return pl.pallas_call(
        functools.partial(_ctx_front_kernel, n_kv=n_kv, qk_norm=qn is not None, seq=seq, n_prev=n_prev,
                          has_sink=sink is not None),
        grid=(steps,),
        in_specs=in_specs,
        out_specs=[pl.BlockSpec((tm, D_MODEL), lambda i: (i, 0)), kv_spec, kv_spec],
        out_shape=[jax.ShapeDtypeStruct((tokens, D_MODEL), BF16), kv_shape, kv_shape],
        compiler_params=_params(("arbitrary",)),
        name="ctx_front",
    )(*args)


def _qkv_proj(x, mods, gain, wq, wk, wv, *, n_kv, dup_kv, qn=None, kn=None, rope=None, seq=None):
    tokens = x.shape[0]
    tm = QKV_TILE
    steps = tokens // tm
    tiles_per_cond = steps // mods.shape[0]
    nk = n_kv * HEAD_DIM
    nk_out = 2 * nk if dup_kv else nk
    in_specs = [
        pl.BlockSpec((tm, D_MODEL), lambda i: (i, 0)),
        pl.BlockSpec((None, ADA_CHUNKS, D_MODEL), lambda i: (i // tiles_per_cond, 0, 0)),
        _const_spec((1, D_MODEL)),
        _const_spec((D_MODEL, D_MODEL)),
        _const_spec((D_MODEL, nk)),
        _const_spec((D_MODEL, nk)),
    ]
    args = [x, mods, gain, wq, wk, wv]
    if qn is not None:
        in_specs += [_const_spec((1, LANES)), _const_spec((1, LANES))]
        args += [qn, kn]
    if rope is not None:
        tiles_per_seq = seq // tm
        in_specs += [pl.BlockSpec((tm, LANES), lambda i: (i % tiles_per_seq, 0))] * 2
        args += list(rope)
    return pl.pallas_call(
        functools.partial(_qkv_kernel, n_kv=n_kv, qk_norm=qn is not None, rope=rope is not None, dup_kv=dup_kv),
        grid=(steps,),
        in_specs=in_specs,
        out_specs=[
            pl.BlockSpec((tm, D_MODEL), lambda i: (i, 0)),
            pl.BlockSpec((tm, nk_out), lambda i: (i, 0)),
            pl.BlockSpec((tm, nk_out), lambda i: (i, 0)),
        ],
        out_shape=[
            jax.ShapeDtypeStruct((tokens, D_MODEL), BF16),
            jax.ShapeDtypeStruct((tokens, nk_out), BF16),
            jax.ShapeDtypeStruct((tokens, nk_out), BF16),
        ],
        compiler_params=_params(("arbitrary",)),
        name="qkv_proj",
    )(*args)


def _mlp_kernel(*refs, final, n_cast, ctx_steps):
    xp_ref, op_ref, xs_ref, os_ref, mods_ref, wo_ref, g_ref, w1_ref, b1_ref, w2_ref, b2_ref = refs[:11]
    pos = 11
    fg_ref = refs[pos] if final else None
    pos += int(final)
    cast_in = refs[pos:pos + n_cast]
    outp_ref, outs_ref = refs[pos + n_cast:pos + n_cast + 2]
    cast_out = refs[pos + n_cast + 2:]

    def update(x_ref, o_ref, out_ref):
        x1 = x_ref[...] + mods_ref[2:3, :] * _dot(o_ref[...], wo_ref[...])
        h = _rmsnorm_rows(x1, g_ref[...]) * (1.0 + mods_ref[4:5, :]) + mods_ref[3:4, :]
        h = h.astype(BF16)
        acc = jnp.zeros_like(x1)
        for c in range(D_FF // MLP_CHUNK):
            sl = slice(c * MLP_CHUNK, (c + 1) * MLP_CHUNK)
            t = jnp.maximum(_dot(h, w1_ref[:, sl]) + b1_ref[:, sl], 0.0)
            acc = acc + _dot((t * t).astype(BF16), w2_ref[sl, :])
        x2 = x1 + mods_ref[5:6, :] * (acc + b2_ref[...])
        if final:
            x2 = _rmsnorm_rows(x2, fg_ref[...])
        out_ref[...] = x2

    @pl.when(pl.program_id(0) < ctx_steps)
    def _():
        update(xp_ref, op_ref, outp_ref)

    @pl.when(pl.program_id(0) >= ctx_steps)
    def _():
        update(xs_ref, os_ref, outs_ref)
        for src, dst in zip(cast_in, cast_out):
            dst[...] = src[...].astype(dst.dtype)


def _mlp_block(xp, op, xs, os_, mods, wo, gain, w1, b1, w2, b2, final_gain=None, to_cast=()):
    tm = TOKEN_TILE
    ctx_steps = xp.shape[0] // tm
    lat_steps = xs.shape[0] // tm
    n_cond = mods.shape[0] - 1
    tiles_per_cond = lat_steps // n_cond

    def ctx_tile(i):
        return jnp.minimum(i, ctx_steps - 1)

    def lat_tile(i):
        return jnp.maximum(i - ctx_steps, 0)

    in_specs = [
        pl.BlockSpec((tm, D_MODEL), lambda i: (ctx_tile(i), 0)),
        pl.BlockSpec((tm, D_MODEL), lambda i: (ctx_tile(i), 0)),
        pl.BlockSpec((tm, D_MODEL), lambda i: (lat_tile(i), 0)),
        pl.BlockSpec((tm, D_MODEL), lambda i: (lat_tile(i), 0)),
        pl.BlockSpec((None, ADA_CHUNKS, D_MODEL),
                     lambda i: (jnp.where(i < ctx_steps, n_cond, lat_tile(i) // tiles_per_cond), 0, 0)),
        _const_spec((D_MODEL, D_MODEL)),
        _const_spec((1, D_MODEL)),
        _const_spec((D_MODEL, D_FF)),
        _const_spec((1, D_FF)),
        _const_spec((D_FF, D_MODEL)),
        _const_spec((1, D_MODEL)),
    ]
    args = [xp, op, xs, os_, mods, wo, gain, w1, b1, w2, b2]
    if final_gain is not None:
        in_specs.append(_const_spec((1, D_MODEL)))
        args.append(final_gain)
    out_specs = [pl.BlockSpec((tm, D_MODEL), lambda i: (ctx_tile(i), 0)),
                 pl.BlockSpec((tm, D_MODEL), lambda i: (lat_tile(i), 0))]
    out_shape = [jax.ShapeDtypeStruct(xp.shape, F32), jax.ShapeDtypeStruct(xs.shape, F32)]
    for w, layer, col0, ncols in to_cast:
        n_layers, rows, cols = w.shape
        slab = rows // lat_steps
        assert rows % lat_steps == 0 and slab % 16 == 0 and col0 % ncols == 0
        in_specs.append(pl.BlockSpec((None, None, slab, ncols),
                                     lambda i, la=layer, cb=col0 // ncols: (la, lat_tile(i), 0, cb)))
        args.append(w.reshape(n_layers, lat_steps, slab, cols))
        out_specs.append(pl.BlockSpec((None, slab, ncols), lambda i: (lat_tile(i), 0, 0)))
        out_shape.append(jax.ShapeDtypeStruct((lat_steps, slab, ncols), BF16))
    outs = pl.pallas_call(
        functools.partial(_mlp_kernel, final=final_gain is not None, n_cast=len(to_cast), ctx_steps=ctx_steps),
        grid=(ctx_steps + lat_steps,),
        in_specs=in_specs,
        out_specs=out_specs,
        out_shape=out_shape,
        compiler_params=_params(("arbitrary",)),
        name="mlp_block",
    )(*args)
    casted = [c.reshape(w.shape[1], ncols) for c, (w, _, _, ncols) in zip(outs[2:], to_cast)]
    return outs[0], outs[1], casted


def _softmax_numerators(scores, extra_logit=None):
    m = functools.reduce(jnp.maximum, [jnp.max(s, axis=-1, keepdims=True) for s in scores])
    if extra_logit is not None:
        m = jnp.maximum(m, extra_logit)
    ps = [jnp.exp2(s - m).astype(BF16) for s in scores]
    return ps, (None if extra_logit is None else jnp.exp2(extra_logit - m))


def _with_ones(vals, value_lanes):
    return jnp.where(value_lanes, vals, jnp.ones_like(vals))


def _stack_heads(qb, low):
    zero = jnp.zeros_like(qb)
    return jnp.concatenate([jnp.where(low, qb, zero), jnp.where(jnp.logical_not(low), qb, zero)], axis=0)


def _values_with_ones(vb, low):
    return jnp.concatenate([_with_ones(vb, low), _with_ones(vb, jnp.logical_not(low))], axis=1)


def _values_with_ones_t(vt_even, vt_odd):
    ones = jnp.ones_like(vt_even)
    return jnp.concatenate([vt_even, ones, ones, vt_odd], axis=0)


def _stacked_pv(ps, v4s, low, extra=None, transposed=None):
    high = jnp.logical_not(low)
    transposed = transposed or (False,) * len(ps)
    acc = functools.reduce(jnp.add, [(_dot_nt if t else _dot)(p, v) for p, v, t in zip(ps, v4s, transposed)])
    half = acc.shape[0] // 2
    outs = []
    for rows, lanes, ones_lanes in ((slice(0, half), slice(0, LANES), high), (slice(half, None), slice(LANES, None), low)):
        part = acc[rows, lanes]
        if extra is not None:
            part = part + jnp.where(ones_lanes, extra[rows], 0.0)
        denom = jnp.max(jnp.where(ones_lanes, part, 0.0), axis=-1, keepdims=True)
        outs.append(part * (1.0 / denom))
    return outs


def _na_group_geometry(rows):
    kh = min(WIN_H, rows)
    n_groups = rows // NA_ROWS
    masked = 2 * WIN_H - 1
    band0 = [int(np.clip(g * NA_ROWS - kh // 2, 0, rows - NA_BAND)) for g in range(n_groups)]
    table_of = [0 if g == 0 else (2 if g == n_groups - 1 else 1) for g in range(n_groups)]
    sel = np.full((3, NA_ROWS, NA_BAND), masked, np.int64)
    for g in (0, 1, n_groups - 1):
        for a in range(NA_ROWS):
            r = g * NA_ROWS + a
            r0 = int(np.clip(r - kh // 2, 0, rows - kh))
            for i in range(NA_BAND):
                kr = band0[g] + i
                if r0 <= kr < r0 + kh:
                    sel[table_of[g], a, i] = kr - r + WIN_H - 1
    return band0, table_of, sel


def _na_attn_kernel(q_ref, k_ref, v_ref, kc_ref, vc_ref, rpb_ref, o_ref, tab_scr, v4_scr, *, rows):
    low = _lane_is_low()
    band0, table_of, sel = _na_group_geometry(rows)
    q_rows = NA_ROWS * GRID_W
    k_rows = NA_BAND * GRID_W
    n_groups = rows // NA_ROWS

    @pl.when(pl.program_id(1) == 0)
    def _():
        qc = lax.broadcasted_iota(jnp.int32, (GRID_W, GRID_W), 0)
        kc = lax.broadcasted_iota(jnp.int32, (GRID_W, GRID_W), 1)
        c0 = jnp.clip(qc - WIN_W // 2, 0, GRID_W - WIN_W)
        col_valid = (kc >= c0) & (kc < c0 + WIN_W)
        for half in range(2):
            blocks = []
            for ri in range(2 * WIN_H - 1):
                row = jnp.broadcast_to(rpb_ref[half, ri:ri + 1, :], (GRID_W, 2 * GRID_W))
                skew = pltpu.roll(row, GRID_W + 1, 1, stride=1, stride_axis=0)
                blocks.append(jnp.where(col_valid, skew[:, :GRID_W], NEG_INF))
            blocks.append(jnp.full((GRID_W, GRID_W), NEG_INF, F32))
            for t in range(3):
                for a in range(NA_ROWS):
                    strip = [blocks[int(sel[t, a, i])] for i in range(NA_BAND)]
                    r = half * q_rows + a * GRID_W
                    tab_scr[t, r:r + GRID_W, :] = jnp.concatenate(strip, axis=1)

    n_batch = q_ref.shape[0]
    kcb = [kc_ref[bb].astype(BF16) for bb in range(n_batch)]
    vct = [vc_ref[bb].astype(BF16) for bb in range(n_batch)]
    vc4 = [_values_with_ones_t(v[:HEAD_DIM], v[HEAD_DIM:]) for v in vct]
    for bb in range(n_batch):
        v4_scr[bb] = _values_with_ones(v_ref[bb], low)

    def scores_of(item):
        bb, g = item
        rq = g * q_rows
        ks = band0[g] * GRID_W
        qs = _stack_heads(q_ref[bb, rq:rq + q_rows, :], low)
        return [_dot_nt(qs, k_ref[bb, ks:ks + k_rows, :]) + tab_scr[table_of[g]], _dot(qs, kcb[bb])]

    items = [(bb, g) for bb in range(n_batch) for g in range(n_groups)]
    nxt = scores_of(items[0])
    for idx, (bb, g) in enumerate(items):
        cur, nxt = nxt, (scores_of(items[idx + 1]) if idx + 1 < len(items) else None)
        ks = band0[g] * GRID_W
        ps, _ = _softmax_numerators(cur)
        lo, hi = _stacked_pv(ps, [v4_scr[bb, ks:ks + k_rows, :], vc4[bb]], low, transposed=(False, True))
        o_ref[bb, g * q_rows:(g + 1) * q_rows, :] = jnp.where(low, lo, hi).astype(o_ref.dtype)


def _na_padded_rpb(rpb):
    pad_l = GRID_W - WIN_W
    return jnp.pad(rpb * LOG2E, ((0, 0), (0, 1), (pad_l, 2 * GRID_W - pad_l - rpb.shape[2])))


def _cache_transposed(cache):
    b, n, past, h, d = cache.shape
    return jnp.transpose(cache, (0, 1, 3, 4, 2)).reshape(b, n, h * d, past)


def _na_attention(q, k, v, cache_k, cache_v, rpb_pad, *, batch, seq, layer_slot):
    rows = seq // GRID_W
    n_pairs = N_HEADS // 2
    q3, k3, v3 = (a.reshape(batch, seq, D_MODEL) for a in (q, k, v))
    past = cache_k.shape[2]
    ck, cv = (_cache_transposed(a) for a in (cache_k, cache_v))
    nb = NA_BATCH
    tok_spec = pl.BlockSpec((nb, seq, LANES), lambda p, b: (b, 0, p))
    ctx_spec = pl.BlockSpec((nb, None, LANES, past), lambda p, b: (b, layer_slot, p, 0))
    out = pl.pallas_call(
        functools.partial(_na_attn_kernel, rows=rows),
        grid=(n_pairs, batch // nb),
        in_specs=[tok_spec, tok_spec, tok_spec, ctx_spec, ctx_spec,
                  pl.BlockSpec((2,) + rpb_pad.shape[1:], lambda p, b: (p, 0, 0))],
        out_specs=tok_spec,
        out_shape=jax.ShapeDtypeStruct((batch, seq, D_MODEL), BF16),
        scratch_shapes=[pltpu.VMEM((3, 2 * NA_ROWS * GRID_W, NA_BAND * GRID_W), F32),
                        pltpu.VMEM((nb, seq, 2 * LANES), BF16)],
        compiler_params=_params(("arbitrary", "arbitrary")),
        name="na_attention",
    )(q3, k3, v3, ck, cv, rpb_pad)
    return out.reshape(batch * seq, D_MODEL)


def _gqa_attn_kernel(*refs, seq, windowed):
    if windowed:
        sink_ref, q_ref, k_ref, v_ref, kc_ref, vc_ref, o_ref, v4_scr, bias_scr = refs
    else:
        q_ref, k_ref, v_ref, kc_ref, vc_ref, o_ref, v4_scr = refs
    kvh = pl.program_id(1)
    low = _lane_is_low()
    high = jnp.logical_not(low)
    band = GQA_BAND
    stacked = GQA_GROUP * Q_BLOCK
    n_blocks = seq // Q_BLOCK

    head_order = list(range(0, GQA_GROUP, 2)) + list(range(1, GQA_GROUP, 2))

    kct = kc_ref[...].astype(BF16)
    vct = vc_ref[...].astype(BF16)
    kcb = jnp.concatenate([kct, kct], axis=0)
    vc4 = _values_with_ones_t(vct, vct)
    v4_scr[...] = _values_with_ones(v_ref[...], low)

    if windowed:
        q_shift = Q_BLOCK.bit_length() - 1
        block_of_row = lax.shift_right_logical(lax.broadcasted_iota(jnp.int32, (stacked, 1), 0), q_shift)
        sink_col = jnp.zeros((stacked, 1), F32)
        for blk_i, g in enumerate(head_order):
            sink_col = jnp.where(block_of_row == blk_i, sink_ref[kvh * GQA_GROUP + g] * LOG2E, sink_col)

        @pl.when((pl.program_id(0) == 0) & (kvh == 0))
        def _():
            q_off = lax.broadcasted_iota(jnp.int32, (stacked, band), 0) & (Q_BLOCK - 1)
            k_off = lax.broadcasted_iota(jnp.int32, (stacked, band), 1)
            for t in range(bias_scr.shape[0]):
                dist = q_off - k_off + t * Q_BLOCK
                bias_scr[t] = jnp.where(jnp.abs(dist) <= WINDOW, 0.0, NEG_INF)

    def key_rows(i):
        if not windowed:
            return slice(None)
        ks = int(np.clip(i * Q_BLOCK - WINDOW, 0, seq - band))
        return slice(ks, ks + band)

    def scores_of(i):
        r0 = i * Q_BLOCK
        qblk = q_ref[r0:r0 + Q_BLOCK, :]
        zero = jnp.zeros((Q_BLOCK, LANES), qblk.dtype)
        parts = []
        for g in head_order:
            blk = qblk[:, (g // 2) * LANES:(g // 2 + 1) * LANES]
            parts.append(jnp.where(low if g % 2 == 0 else high, blk, zero))
        qs = jnp.concatenate(parts, axis=0)
        keys = key_rows(i)
        s_lat = _dot_nt(qs, k_ref[keys, :])
        if windowed:
            s_lat = s_lat + bias_scr[(r0 - keys.start) // Q_BLOCK]
        return [s_lat, _dot(qs, kcb)]

    nxt = scores_of(0)
    for i in range(n_blocks):
        cur, nxt = nxt, (scores_of(i + 1) if i + 1 < n_blocks else None)
        ps, extra = _softmax_numerators(cur, sink_col if windowed else None)
        even, odd = _stacked_pv(ps, [v4_scr[key_rows(i), :], vc4], low, extra, transposed=(False, True))
        r0 = i * Q_BLOCK
        for c in range(GQA_GROUP // 2):
            rows_c = slice(c * Q_BLOCK, (c + 1) * Q_BLOCK)
            o_ref[r0:r0 + Q_BLOCK, c * LANES:(c + 1) * LANES] = jnp.where(low, even[rows_c], odd[rows_c]).astype(o_ref.dtype)


def _gqa_attention(q, kd, vd, cache_k, cache_v, *, batch, seq, layer_slot, sink=None):
    group_w = GQA_GROUP * HEAD_DIM
    q3 = q.reshape(batch, seq, D_MODEL)
    kd3 = kd.reshape(batch, seq, N_KV_GQA * LANES)
    vd3 = vd.reshape(batch, seq, N_KV_GQA * LANES)
    past = cache_k.shape[2]
    ck, cv = (_cache_transposed(a) for a in (cache_k, cache_v))
    q_spec = pl.BlockSpec((None, seq, group_w), lambda b, h: (b, 0, h))
    kv_spec = pl.BlockSpec((None, seq, LANES), lambda b, h: (b, 0, h))
    ctx_spec = pl.BlockSpec((None, None, HEAD_DIM, past), lambda b, h: (b, layer_slot, h, 0))
    in_specs = [q_spec, kv_spec, kv_spec, ctx_spec, ctx_spec]
    args = [q3, kd3, vd3, ck, cv]
    scratch = [pltpu.VMEM((seq, 2 * LANES), BF16)]
    if sink is not None:
        in_specs.insert(0, pl.BlockSpec(memory_space=pltpu.SMEM))
        args.insert(0, sink)
        n_offsets = (GQA_BAND - Q_BLOCK) // Q_BLOCK + 1
        scratch.append(pltpu.VMEM((n_offsets, GQA_GROUP * Q_BLOCK, GQA_BAND), F32))
    out = pl.pallas_call(
        functools.partial(_gqa_attn_kernel, seq=seq, windowed=sink is not None),
        grid=(batch, N_KV_GQA),
        in_specs=in_specs,
        out_specs=q_spec,
        out_shape=jax.ShapeDtypeStruct((batch, seq, D_MODEL), BF16),
        scratch_shapes=scratch,
        compiler_params=_params(("arbitrary", "arbitrary")),
        name="gqa_attention",
    )(*args)
    return out.reshape(batch * seq, D_MODEL)


def _rope_tables(seq):
    t = jnp.arange(seq)
    rows = (t // GRID_W).astype(F32)
    cols = (t % GRID_W).astype(F32)
    quarter = HEAD_DIM // 4
    freqs = jnp.exp(-math.log(ROPE_BASE) * jnp.arange(quarter, dtype=F32) / quarter)
    ang_r = rows[:, None] * freqs[None, :]
    ang_c = cols[:, None] * freqs[None, :]
    cos = jnp.concatenate([jnp.cos(ang_r)] * 2 + [jnp.cos(ang_c)] * 2, axis=1)
    sin = jnp.concatenate([-jnp.sin(ang_r), jnp.sin(ang_r), -jnp.sin(ang_c), jnp.sin(ang_c)], axis=1)
    return jnp.tile(cos, (1, LANES // HEAD_DIM)), jnp.tile(sin, (1, LANES // HEAD_DIM))


def kernel(x_prompt, x_sample, cache_k_a, cache_v_a, cache_k_b, cache_v_b, cache_k_c, cache_v_c, c, c_ctx,
           ada_w, ada_b, norm_mix_g, norm_mlp_g, w_o, mlp_w1, mlp_b1, mlp_w2, mlp_b2, w_qkv_a, rpb_a,
           w_qkv_b, sink_b, w_qkv_c, q_norm_c, k_norm_c, final_norm_g):
    batch, seq, _ = x_prompt.shape
    dec_batch, dec_seq, _ = x_sample.shape
    depth = ada_w.shape[0]
    assert dec_batch + 1 <= COND_ROWS and dec_seq % GRID_W == 0
    assert (dec_seq // GRID_W) % NA_ROWS == 0 and dec_seq // GRID_W >= 3 * NA_ROWS

    cond = jnp.concatenate([c, c_ctx[None, :], jnp.zeros((COND_ROWS - dec_batch - 1, D_MODEL), F32)], axis=0)
    mods = _ada_mods(cond, ada_w, ada_b).reshape(depth, COND_ROWS, ADA_CHUNKS, D_MODEL)

    w_qkv = (w_qkv_a, w_qkv_b, w_qkv_c)
    n_kv = (N_KV_A, N_KV_GQA, N_KV_GQA)
    caches_k = (cache_k_a, cache_k_b, cache_k_c)
    caches_v = (cache_v_a, cache_v_b, cache_v_c)
    new_kv = [None] * N_MIXERS
    rope = _rope_tables(dec_seq)
    nq = N_HEADS * HEAD_DIM

    xp = x_prompt.reshape(batch * seq, D_MODEL)
    xs = x_sample.reshape(dec_batch * dec_seq, D_MODEL)

    def layer_weights_f32(l):
        m, j = l % N_MIXERS, l // N_MIXERS
        nk = n_kv[m] * HEAD_DIM
        return [(w_qkv[m], j, 0, nq), (w_qkv[m], j, nq, nk), (w_qkv[m], j, nq + nk, nk), (w_o, l, 0, D_MODEL),
                (mlp_w1, l, 0, D_FF), (mlp_w2, l, 0, D_MODEL)]

    weights = [w[la, :, c0:c0 + nc].astype(BF16) for w, la, c0, nc in layer_weights_f32(0)]

    for l in range(depth):
        m, j = l % N_MIXERS, l // N_MIXERS
        wq, wk, wv, wo, w1, w2 = weights
        b1 = mlp_b1[l][None, :]
        b2 = mlp_b2[l][None, :]
        g_mix = norm_mix_g[l][None, :]
        g_mlp = norm_mlp_g[l][None, :]
        mods_ctx = mods[l, dec_batch:dec_batch + 1]
        mods_lat = mods[l, :dec_batch]
        final = final_norm_g[None, :] if l == depth - 1 else None
        norm_args, ctx_norm_args = {}, {}
        if m == 2:
            qn = jnp.tile(q_norm_c[j], LANES // HEAD_DIM)[None, :]
            norm_args = dict(qn=qn, kn=jnp.tile(k_norm_c[j], LANES // HEAD_DIM)[None, :])
            ctx_norm_args = dict(qn=qn, kn_col=jnp.tile(k_norm_c[j], n_kv[m])[:, None])
        sink = sink_b[j] if m == 1 else None

        o_ctx, kt, vt = _ctx_front(xp, mods_ctx, g_mix, wq, wk.T, wv.T, n_kv=n_kv[m], seq=seq, prev=new_kv[m],
                                   sink=sink, **ctx_norm_args)
        new_kv[m] = (kt, vt)

        if m == 0:
            q, k, v = _qkv_proj(xs, mods_lat, g_mix, wq, wk, wv, n_kv=n_kv[m], dup_kv=False)
            o_lat = _na_attention(q, k, v, caches_k[m], caches_v[m], _na_padded_rpb(rpb_a[j]),
                                  batch=dec_batch, seq=dec_seq, layer_slot=j)
        else:
            q, k, v = _qkv_proj(xs, mods_lat, g_mix, wq, wk, wv, n_kv=n_kv[m], dup_kv=True,
                                rope=rope, seq=dec_seq, **norm_args)
            o_lat = _gqa_attention(q, k, v, caches_k[m], caches_v[m], batch=dec_batch, seq=dec_seq, layer_slot=j,
                                   sink=sink)

        to_cast = layer_weights_f32(l + 1) if l + 1 < depth else ()
        xp, xs, weights = _mlp_block(xp, o_ctx, xs, o_lat, mods[l, :dec_batch + 1], wo, g_mlp, w1, b1, w2, b2, final,
                                     to_cast=to_cast)

    y_prompt = xp.reshape(batch, seq, D_MODEL)
    y_sample = xs.reshape(dec_batch, dec_seq, D_MODEL)
    outs = [y_prompt, y_sample]
    for m in range(N_MIXERS):
        for stacked in new_kv[m]:
            t = stacked.reshape(batch, stacked.shape[1], n_kv[m], HEAD_DIM, seq)
            outs.append(jnp.transpose(t, (0, 1, 4, 2, 3)))
    return tuple(outs)
```

```python
import functools
import math

import numpy as np
import jax
import jax.numpy as jnp
from jax import lax
from jax.experimental import pallas as pl
from jax.experimental.pallas import tpu as pltpu

D_MODEL = 1024
N_HEADS = 16
HEAD_DIM = 64
N_KV_A = 16
N_KV_GQA = 4
GQA_GROUP = N_HEADS // N_KV_GQA
D_FF = 4 * D_MODEL
ADA_CHUNKS = 6
N_MIXERS = 3
GRID_W = 64
WINDOW = 128
WIN_H = 8
WIN_W = 16
ROPE_BASE = 10000.0
EPS = 1e-6
NEG_INF = -1e30
LOG2E = math.log2(math.e)
Q_SCALE = HEAD_DIM ** -0.5 * LOG2E

HEAD_SHIFT = HEAD_DIM.bit_length() - 1
LANES = 128
COND_ROWS = 16
TOKEN_TILE = 512
MLP_CHUNK = 1024
QKV_TILE = 1024
QKV_SUB_TILE = 512
NA_ROWS = 4
NA_BAND = NA_ROWS + WIN_H
NA_BATCH = 2
Q_BLOCK = 128
GQA_BAND = Q_BLOCK + 2 * WINDOW
VMEM_LIMIT = 56 * 1024 * 1024

BF16 = jnp.bfloat16
F32 = jnp.float32


def _dot(a, b):
    return jnp.dot(a, b, preferred_element_type=F32)


def _dot_nt(a, b):
    return lax.dot_general(a, b, (((1,), (1,)), ((), ())), preferred_element_type=F32)


def _rmsnorm_rows(x, g):
    return x * lax.rsqrt(jnp.mean(x * x, axis=-1, keepdims=True) + EPS) * g


def _lane_is_low(shape=(1, LANES)):
    return lax.broadcasted_iota(jnp.int32, shape, len(shape) - 1) < HEAD_DIM


def _const_spec(shape):
    return pl.BlockSpec(shape, lambda *_: (0,) * len(shape), pipeline_mode=pl.Buffered(1))


def _params(semantics):
    return pltpu.CompilerParams(dimension_semantics=semantics, vmem_limit_bytes=VMEM_LIMIT)


def _cast_operands(to_cast, steps, step_of):
    in_specs, args, out_specs, out_shape = [], [], [], []
    for w, layer, col0, ncols in to_cast:
        n_layers, rows, cols = w.shape
        slab = rows // steps
        assert rows % steps == 0 and slab % 16 == 0 and col0 % ncols == 0
        in_specs.append(pl.BlockSpec((None, None, slab, ncols),
                                     lambda i, la=layer, cb=col0 // ncols: (la, step_of(i), 0, cb)))
        args.append(w.reshape(n_layers, steps, slab, cols))
        out_specs.append(pl.BlockSpec((None, slab, ncols), lambda i: (step_of(i), 0, 0)))
        out_shape.append(jax.ShapeDtypeStruct((steps, slab, ncols), BF16))
    return in_specs, args, out_specs, out_shape


def _ada_kernel(c_ref, w_ref, b_ref, o_ref):
    c = c_ref[...]
    s = (c / (1.0 + jnp.exp(-c))).astype(BF16)
    o_ref[...] = _dot(s, w_ref[...].astype(BF16)) + b_ref[...]


def _ada_mods(cond, ada_w, ada_b):
    depth, _, n_out = ada_w.shape
    nb = n_out // D_MODEL
    return pl.pallas_call(
        _ada_kernel,
        grid=(depth, nb),
        in_specs=[
            pl.BlockSpec((COND_ROWS, D_MODEL), lambda l, n: (0, 0)),
            pl.BlockSpec((None, D_MODEL, D_MODEL), lambda l, n: (l, 0, n)),
            pl.BlockSpec((None, 1, D_MODEL), lambda l, n: (l, 0, n)),
        ],
        out_specs=pl.BlockSpec((None, COND_ROWS, D_MODEL), lambda l, n: (l, 0, n)),
        out_shape=jax.ShapeDtypeStruct((depth, COND_ROWS, n_out), F32),
        compiler_params=_params(("arbitrary", "arbitrary")),
        name="ada_mods",
    )(cond, ada_w, ada_b.reshape(depth, 1, n_out))


def _head_rmsnorm(blk, gain):
    sq = blk * blk
    hi = sq.astype(BF16)
    lo = (sq - hi.astype(F32)).astype(BF16)
    r = lax.shift_right_logical(lax.broadcasted_iota(jnp.int32, (LANES, LANES), 0), HEAD_SHIFT)
    c = lax.shift_right_logical(lax.broadcasted_iota(jnp.int32, (LANES, LANES), 1), HEAD_SHIFT)
    avg = jnp.where(r == c, 1.0 / HEAD_DIM, 0.0).astype(BF16)
    ms = _dot(hi, avg) + _dot(lo, avg)
    return blk * lax.rsqrt(ms + EPS) * gain


def _rope_block(blk, cos, sin_signed):
    lane = lax.broadcasted_iota(jnp.int32, (1, LANES), 1)
    first = (lane & 16) == 0
    partner = jnp.where(first, pltpu.roll(blk, LANES - 16, 1), pltpu.roll(blk, 16, 1))
    return blk * cos + partner * sin_signed


def _qkv_kernel(*refs, n_kv, qk_norm, rope, dup_kv, n_cast):
    x_ref, mods_ref, g_ref, wq_ref, wk_ref, wv_ref = refs[:6]
    pos = 6
    if qk_norm:
        qn_ref, kn_ref = refs[pos:pos + 2]
        pos += 2
    if rope:
        cos_ref, sin_ref = refs[pos:pos + 2]
        pos += 2
    cast_in = refs[pos:pos + n_cast]
    pos += n_cast
    q_ref, k_ref, v_ref = refs[pos:pos + 3]
    cast_out = refs[pos + 3:]
    low = _lane_is_low()
    n_sub = x_ref.shape[0] // QKV_SUB_TILE

    def project(i):
        rows = slice(i * QKV_SUB_TILE, (i + 1) * QKV_SUB_TILE)
        h = _rmsnorm_rows(x_ref[rows, :], g_ref[...]) * (1.0 + mods_ref[1:2, :]) + mods_ref[0:1, :]
        h = h.astype(BF16)
        return _dot(h, wq_ref[...]), _dot(h, wk_ref[...]), _dot(h, wv_ref[...])

    def finish(i, qkv):
        q, k, v = qkv
        rows = slice(i * QKV_SUB_TILE, (i + 1) * QKV_SUB_TILE)

        def post(blk, gain_ref):
            if qk_norm:
                blk = _head_rmsnorm(blk, gain_ref[...])
            if rope:
                blk = _rope_block(blk, cos_ref[rows, :], sin_ref[rows, :])
            return blk

        for cb in range(N_HEADS * HEAD_DIM // LANES):
            sl = slice(cb * LANES, (cb + 1) * LANES)
            blk = post(q[:, sl], qn_ref if qk_norm else None)
            q_ref[rows, sl] = (blk * Q_SCALE).astype(q_ref.dtype)

        for cb in range(n_kv * HEAD_DIM // LANES):
            sl = slice(cb * LANES, (cb + 1) * LANES)
            kb = post(k[:, sl], kn_ref if qk_norm else None)
            vb = v[:, sl]
            if dup_kv:
                for arr, ref in ((kb, k_ref), (vb, v_ref)):
                    rolled = pltpu.roll(arr, HEAD_DIM, 1)
                    ref[rows, (2 * cb) * LANES:(2 * cb + 1) * LANES] = jnp.where(low, arr, rolled).astype(ref.dtype)
                    ref[rows, (2 * cb + 1) * LANES:(2 * cb + 2) * LANES] = jnp.where(low, rolled, arr).astype(ref.dtype)
            else:
                k_ref[rows, sl] = kb.astype(k_ref.dtype)
                v_ref[rows, sl] = vb.astype(v_ref.dtype)

    nxt = project(0)
    for i in range(n_sub):
        cur, nxt = nxt, (project(i + 1) if i + 1 < n_sub else None)
        finish(i, cur)
    for src, dst in zip(cast_in, cast_out):
        dst[...] = src[...].astype(dst.dtype)


def _ctx_front_kernel(*refs, n_kv, qk_norm, seq, n_prev, has_sink):
    if has_sink:
        sink_ref, refs = refs[0], refs[1:]
    x_ref, mods_ref, g_ref, wq_ref, wkt_ref, wvt_ref = refs[:6]
    pos = 6
    if qk_norm:
        qn_ref, kn_ref = refs[pos:pos + 2]
        pos += 2
    if n_prev:
        kprev_ref, vprev_ref = refs[pos:pos + 2]
    o_ref, kt_ref, vt_ref = refs[-3:]
    low = _lane_is_low()
    n_pairs = N_HEADS // 2
    n_sub = x_ref.shape[0] // seq
    ones = jnp.ones((HEAD_DIM, seq), BF16)

    if n_prev:
        kt_ref[:, :n_prev] = kprev_ref[...]
        vt_ref[:, :n_prev] = vprev_ref[...]

    def head_rows(h):
        kvh = h * n_kv // N_HEADS
        return slice(kvh * HEAD_DIM, (kvh + 1) * HEAD_DIM)

    def project(b):
        rows = slice(b * seq, (b + 1) * seq)
        h = _rmsnorm_rows(x_ref[rows, :], g_ref[...]) * (1.0 + mods_ref[1:2, :]) + mods_ref[0:1, :]
        h = h.astype(BF16)
        q = _dot(h, wq_ref[...])
        q_pairs = []
        for cb in range(n_pairs):
            blk = q[:, cb * LANES:(cb + 1) * LANES]
            if qk_norm:
                blk = _head_rmsnorm(blk, qn_ref[...])
            q_pairs.append((blk * Q_SCALE).astype(BF16))
        kt = _dot_nt(wkt_ref[...], h)
        vt = _dot_nt(wvt_ref[...], h)
        if qk_norm:
            k3 = kt.reshape(n_kv, HEAD_DIM, seq)
            k3 = k3 * lax.rsqrt(jnp.mean(k3 * k3, axis=1, keepdims=True) + EPS)
            kt = k3.reshape(kt.shape) * kn_ref[...]
        kt_ref[b, n_prev] = kt
        vt_ref[b, n_prev] = vt
        return q_pairs, kt.astype(BF16), vt.astype(BF16)

    def attend(b, projected):
        q_pairs, kt, vt = projected
        rows = slice(b * seq, (b + 1) * seq)

        def scores_of(p):
            ktp = jnp.concatenate([kt[head_rows(2 * p), :], kt[head_rows(2 * p + 1), :]], axis=0)
            return [_dot(_stack_heads(q_pairs[p], low), ktp)]

        nxt = scores_of(0)
        for p in range(n_pairs):
            cur, nxt = nxt, (scores_of(p + 1) if p + 1 < n_pairs else None)
            sink_col = None
            if has_sink:
                first_head = lax.broadcasted_iota(jnp.int32, (2 * seq, 1), 0) < seq
                sink_col = jnp.where(first_head, sink_ref[2 * p], sink_ref[2 * p + 1]) * LOG2E
            ps, extra = _softmax_numerators(cur, sink_col)
            v4t = jnp.concatenate([vt[head_rows(2 * p), :], ones, ones, vt[head_rows(2 * p + 1), :]], axis=0)
            even, odd = _stacked_pv(ps, [v4t], low, extra, transposed=(True,))
            o_ref[rows, p * LANES:(p + 1) * LANES] = jnp.where(low, even, odd).astype(o_ref.dtype)

    nxt = project(0)
    for b in range(n_sub):
        cur, nxt = nxt, (project(b + 1) if b + 1 < n_sub else None)
        attend(b, cur)


def _ctx_front(x, mods, gain, wq, wkt, wvt, *, n_kv, seq, qn=None, kn_col=None, prev=None, sink=None):
    tokens = x.shape[0]
    tm = TOKEN_TILE
    steps = tokens // tm
    nk = n_kv * HEAD_DIM
    n_prev = 0 if prev is None else prev[0].shape[1]
    in_specs = [
        pl.BlockSpec((tm, D_MODEL), lambda i: (i, 0)),
        pl.BlockSpec((None, ADA_CHUNKS, D_MODEL), lambda i: (0, 0, 0)),
        _const_spec((1, D_MODEL)),
        _const_spec((D_MODEL, D_MODEL)),
        _const_spec((nk, D_MODEL)),
        _const_spec((nk, D_MODEL)),
    ]
    args = [x, mods, gain, wq, wkt, wvt]
    if qn is not None:
        in_specs += [_const_spec((1, LANES)), _const_spec((nk, 1))]
        args += [qn, kn_col]
    if n_prev:
        in_specs += [pl.BlockSpec((tm // seq, n_prev, nk, seq), lambda i: (i, 0, 0, 0))] * 2
        args += list(prev)
    if sink is not None:
        in_specs.insert(0, pl.BlockSpec(memory_space=pltpu.SMEM))
        args.insert(0, sink)
    kv_spec = pl.BlockSpec((tm // seq, n_prev + 1, nk, seq), lambda i: (i, 0, 0, 0))
    kv_shape = jax.ShapeDtypeStruct((tokens // seq, n_prev + 1, nk, seq), F32)
    return pl.pallas_call(
        functools.partial(_ctx_front_kernel, n_kv=n_kv, qk_norm=qn is not None, seq=seq, n_prev=n_prev,
                          has_sink=sink is not None),
        grid=(steps,),
        in_specs=in_specs,
        out_specs=[pl.BlockSpec((tm, D_MODEL), lambda i: (i, 0)), kv_spec, kv_spec],
        out_shape=[jax.ShapeDtypeStruct((tokens, D_MODEL), BF16), kv_shape, kv_shape],
        compiler_params=_params(("arbitrary",)),
        name="ctx_front",
    )(*args)


def _qkv_proj(x, mods, gain, wq, wk, wv, *, n_kv, dup_kv, qn=None, kn=None, rope=None, seq=None, to_cast=()):
    tokens = x.shape[0]
    tm = QKV_TILE
    steps = tokens // tm
    tiles_per_cond = steps // mods.shape[0]
    nk = n_kv * HEAD_DIM
    nk_out = 2 * nk if dup_kv else nk
    in_specs = [
        pl.BlockSpec((tm, D_MODEL), lambda i: (i, 0)),
        pl.BlockSpec((None, ADA_CHUNKS, D_MODEL), lambda i: (i // tiles_per_cond, 0, 0)),
        _const_spec((1, D_MODEL)),
        _const_spec((D_MODEL, D_MODEL)),
        _const_spec((D_MODEL, nk)),
        _const_spec((D_MODEL, nk)),
    ]
    args = [x, mods, gain, wq, wk, wv]
    if qn is not None:
        in_specs += [_const_spec((1, LANES)), _const_spec((1, LANES))]
        args += [qn, kn]
    if rope is not None:
        tiles_per_seq = seq // tm
        in_specs += [pl.BlockSpec((tm, LANES), lambda i: (i % tiles_per_seq, 0))] * 2
        args += list(rope)
    cast_in, cast_args, cast_out, cast_shape = _cast_operands(to_cast, steps, lambda i: i)
    outs = pl.pallas_call(
        functools.partial(_qkv_kernel, n_kv=n_kv, qk_norm=qn is not None, rope=rope is not None, dup_kv=dup_kv,
                          n_cast=len(to_cast)),
        grid=(steps,),
        in_specs=in_specs + cast_in,
        out_specs=[
            pl.BlockSpec((tm, D_MODEL), lambda i: (i, 0)),
            pl.BlockSpec((tm, nk_out), lambda i: (i, 0)),
            pl.BlockSpec((tm, nk_out), lambda i: (i, 0)),
        ] + cast_out,
        out_shape=[
            jax.ShapeDtypeStruct((tokens, D_MODEL), BF16),
            jax.ShapeDtypeStruct((tokens, nk_out), BF16),
            jax.ShapeDtypeStruct((tokens, nk_out), BF16),
        ] + cast_shape,
        compiler_params=_params(("arbitrary",)),
        name="qkv_proj",
    )(*args, *cast_args)
    casted = [c.reshape(w.shape[1], ncols) for c, (w, _, _, ncols) in zip(outs[3:], to_cast)]
    return outs[0], outs[1], outs[2], casted


def _mlp_kernel(*refs, final, n_cast, ctx_steps):
    xp_ref, op_ref, xs_ref, os_ref, mods_ref, wo_ref, g_ref, w1_ref, b1_ref, w2_ref, b2_ref = refs[:11]
    pos = 11
    fg_ref = refs[pos] if final else None
    pos += int(final)
    cast_in = refs[pos:pos + n_cast]
    outp_ref, outs_ref = refs[pos + n_cast:pos + n_cast + 2]
    cast_out = refs[pos + n_cast + 2:]

    def update(x_ref, o_ref, out_ref):
        x1 = x_ref[...] + mods_ref[2:3, :] * _dot(o_ref[...], wo_ref[...])
        h = _rmsnorm_rows(x1, g_ref[...]) * (1.0 + mods_ref[4:5, :]) + mods_ref[3:4, :]
        h = h.astype(BF16)
        acc = jnp.zeros_like(x1)
        for c in range(D_FF // MLP_CHUNK):
            sl = slice(c * MLP_CHUNK, (c + 1) * MLP_CHUNK)
            t = jnp.maximum(_dot(h, w1_ref[:, sl]) + b1_ref[:, sl], 0.0)
            acc = acc + _dot((t * t).astype(BF16), w2_ref[sl, :])
        x2 = x1 + mods_ref[5:6, :] * (acc + b2_ref[...])
        if final:
            x2 = _rmsnorm_rows(x2, fg_ref[...])
        out_ref[...] = x2

    @pl.when(pl.program_id(0) < ctx_steps)
    def _():
        update(xp_ref, op_ref, outp_ref)

    @pl.when(pl.program_id(0) >= ctx_steps)
    def _():
        update(xs_ref, os_ref, outs_ref)
        for src, dst in zip(cast_in, cast_out):
            dst[...] = src[...].astype(dst.dtype)


def _mlp_block(xp, op, xs, os_, mods, wo, gain, w1, b1, w2, b2, final_gain=None, to_cast=()):
    tm = TOKEN_TILE
    ctx_steps = xp.shape[0] // tm
    lat_steps = xs.shape[0] // tm
    n_cond = mods.shape[0] - 1
    tiles_per_cond = lat_steps // n_cond

    def ctx_tile(i):
        return jnp.minimum(i, ctx_steps - 1)

    def lat_tile(i):
        return jnp.maximum(i - ctx_steps, 0)

    in_specs = [
        pl.BlockSpec((tm, D_MODEL), lambda i: (ctx_tile(i), 0)),
        pl.BlockSpec((tm, D_MODEL), lambda i: (ctx_tile(i), 0)),
        pl.BlockSpec((tm, D_MODEL), lambda i: (lat_tile(i), 0)),
        pl.BlockSpec((tm, D_MODEL), lambda i: (lat_tile(i), 0)),
        pl.BlockSpec((None, ADA_CHUNKS, D_MODEL),
                     lambda i: (jnp.where(i < ctx_steps, n_cond, lat_tile(i) // tiles_per_cond), 0, 0)),
        _const_spec((D_MODEL, D_MODEL)),
        _const_spec((1, D_MODEL)),
        _const_spec((D_MODEL, D_FF)),
        _const_spec((1, D_FF)),
        _const_spec((D_FF, D_MODEL)),
        _const_spec((1, D_MODEL)),
    ]
    args = [xp, op, xs, os_, mods, wo, gain, w1, b1, w2, b2]
    if final_gain is not None:
        in_specs.append(_const_spec((1, D_MODEL)))
        args.append(final_gain)
    out_specs = [pl.BlockSpec((tm, D_MODEL), lambda i: (ctx_tile(i), 0)),
                 pl.BlockSpec((tm, D_MODEL), lambda i: (lat_tile(i), 0))]
    out_shape = [jax.ShapeDtypeStruct(xp.shape, F32), jax.ShapeDtypeStruct(xs.shape, F32)]
    cast_in, cast_args, cast_out, cast_shape = _cast_operands(to_cast, lat_steps, lat_tile)
    in_specs += cast_in
    args += cast_args
    out_specs += cast_out
    out_shape += cast_shape
    outs = pl.pallas_call(
        functools.partial(_mlp_kernel, final=final_gain is not None, n_cast=len(to_cast), ctx_steps=ctx_steps),
        grid=(ctx_steps + lat_steps,),
        in_specs=in_specs,
        out_specs=out_specs,
        out_shape=out_shape,
        compiler_params=_params(("arbitrary",)),
        name="mlp_block",
    )(*args)
    casted = [c.reshape(w.shape[1], ncols) for c, (w, _, _, ncols) in zip(outs[2:], to_cast)]
    return outs[0], outs[1], casted


def _softmax_numerators(scores, extra_logit=None):
    m = functools.reduce(jnp.maximum, [jnp.max(s, axis=-1, keepdims=True) for s in scores])
    if extra_logit is not None:
        m = jnp.maximum(m, extra_logit)
    ps = [jnp.exp2(s - m).astype(BF16) for s in scores]
    return ps, (None if extra_logit is None else jnp.exp2(extra_logit - m))


def _with_ones(vals, value_lanes):
    return jnp.where(value_lanes, vals, jnp.ones_like(vals))


def _stack_heads(qb, low):
    zero = jnp.zeros_like(qb)
    return jnp.concatenate([jnp.where(low, qb, zero), jnp.where(jnp.logical_not(low), qb, zero)], axis=0)


def _values_with_ones(vb, low):
    return jnp.concatenate([_with_ones(vb, low), _with_ones(vb, jnp.logical_not(low))], axis=1)


def _values_with_ones_t(vt_even, vt_odd):
    ones = jnp.ones_like(vt_even)
    return jnp.concatenate([vt_even, ones, ones, vt_odd], axis=0)


def _stacked_pv(ps, v4s, low, extra=None, transposed=None):
    high = jnp.logical_not(low)
    transposed = transposed or (False,) * len(ps)
    acc = functools.reduce(jnp.add, [(_dot_nt if t else _dot)(p, v) for p, v, t in zip(ps, v4s, transposed)])
    half = acc.shape[0] // 2
    outs = []
    for rows, lanes, ones_lanes in ((slice(0, half), slice(0, LANES), high), (slice(half, None), slice(LANES, None), low)):
        part = acc[rows, lanes]
        if extra is not None:
            part = part + jnp.where(ones_lanes, extra[rows], 0.0)
        denom = jnp.max(jnp.where(ones_lanes, part, 0.0), axis=-1, keepdims=True)
        outs.append(part * (1.0 / denom))
    return outs


def _na_group_geometry(rows):
    kh = min(WIN_H, rows)
    n_groups = rows // NA_ROWS
    masked = 2 * WIN_H - 1
    band0 = [int(np.clip(g * NA_ROWS - kh // 2, 0, rows - NA_BAND)) for g in range(n_groups)]
    table_of = [0 if g == 0 else (2 if g == n_groups - 1 else 1) for g in range(n_groups)]
    sel = np.full((3, NA_ROWS, NA_BAND), masked, np.int64)
    for g in (0, 1, n_groups - 1):
        for a in range(NA_ROWS):
            r = g * NA_ROWS + a
            r0 = int(np.clip(r - kh // 2, 0, rows - kh))
            for i in range(NA_BAND):
                kr = band0[g] + i
                if r0 <= kr < r0 + kh:
                    sel[table_of[g], a, i] = kr - r + WIN_H - 1
    return band0, table_of, sel


def _na_attn_kernel(q_ref, k_ref, v_ref, kc_ref, vc_ref, rpb_ref, o_ref, tab_scr, v4_scr, *, rows):
    low = _lane_is_low()
    band0, table_of, sel = _na_group_geometry(rows)
    q_rows = NA_ROWS * GRID_W
    k_rows = NA_BAND * GRID_W
    n_groups = rows // NA_ROWS

    @pl.when(pl.program_id(1) == 0)
    def _():
        qc = lax.broadcasted_iota(jnp.int32, (GRID_W, GRID_W), 0)
        kc = lax.broadcasted_iota(jnp.int32, (GRID_W, GRID_W), 1)
        c0 = jnp.clip(qc - WIN_W // 2, 0, GRID_W - WIN_W)
        col_valid = (kc >= c0) & (kc < c0 + WIN_W)
        for half in range(2):
            blocks = []
            for ri in range(2 * WIN_H - 1):
                row = jnp.broadcast_to(rpb_ref[half, ri:ri + 1, :], (GRID_W, 2 * GRID_W))
                skew = pltpu.roll(row, GRID_W + 1, 1, stride=1, stride_axis=0)
                blocks.append(jnp.where(col_valid, skew[:, :GRID_W], NEG_INF))
            blocks.append(jnp.full((GRID_W, GRID_W), NEG_INF, F32))
            for t in range(3):
                for a in range(NA_ROWS):
                    strip = [blocks[int(sel[t, a, i])] for i in range(NA_BAND)]
                    r = half * q_rows + a * GRID_W
                    tab_scr[t, r:r + GRID_W, :] = jnp.concatenate(strip, axis=1)

    n_batch = q_ref.shape[0]
    kcb = [kc_ref[bb].astype(BF16) for bb in range(n_batch)]
    vct = [vc_ref[bb].astype(BF16) for bb in range(n_batch)]
    vc4 = [_values_with_ones_t(v[:HEAD_DIM], v[HEAD_DIM:]) for v in vct]
    for bb in range(n_batch):
        v4_scr[bb] = _values_with_ones(v_ref[bb], low)

    def scores_of(item):
        bb, g = item
        rq = g * q_rows
        ks = band0[g] * GRID_W
        qs = _stack_heads(q_ref[bb, rq:rq + q_rows, :], low)
        return [_dot_nt(qs, k_ref[bb, ks:ks + k_rows, :]) + tab_scr[table_of[g]], _dot(qs, kcb[bb])]

    items = [(bb, g) for bb in range(n_batch) for g in range(n_groups)]
    nxt = scores_of(items[0])
    for idx, (bb, g) in enumerate(items):
        cur, nxt = nxt, (scores_of(items[idx + 1]) if idx + 1 < len(items) else None)
        ks = band0[g] * GRID_W
        ps, _ = _softmax_numerators(cur)
        lo, hi = _stacked_pv(ps, [v4_scr[bb, ks:ks + k_rows, :], vc4[bb]], low, transposed=(False, True))
        o_ref[bb, g * q_rows:(g + 1) * q_rows, :] = jnp.where(low, lo, hi).astype(o_ref.dtype)


def _na_padded_rpb(rpb):
    pad_l = GRID_W - WIN_W
    return jnp.pad(rpb * LOG2E, ((0, 0), (0, 1), (pad_l, 2 * GRID_W - pad_l - rpb.shape[2])))


def _cache_transposed(cache):
    b, n, past, h, d = cache.shape
    return jnp.transpose(cache, (0, 1, 3, 4, 2)).reshape(b, n, h * d, past)


def _na_attention(q, k, v, cache_k, cache_v, rpb_pad, *, batch, seq, layer_slot):
    rows = seq // GRID_W
    n_pairs = N_HEADS // 2
    q3, k3, v3 = (a.reshape(batch, seq, D_MODEL) for a in (q, k, v))
    past = cache_k.shape[2]
    ck, cv = (_cache_transposed(a) for a in (cache_k, cache_v))
    nb = NA_BATCH
    tok_spec = pl.BlockSpec((nb, seq, LANES), lambda p, b: (b, 0, p))
    ctx_spec = pl.BlockSpec((nb, None, LANES, past), lambda p, b: (b, layer_slot, p, 0))
    out = pl.pallas_call(
        functools.partial(_na_attn_kernel, rows=rows),
        grid=(n_pairs, batch // nb),
        in_specs=[tok_spec, tok_spec, tok_spec, ctx_spec, ctx_spec,
                  pl.BlockSpec((2,) + rpb_pad.shape[1:], lambda p, b: (p, 0, 0))],
        out_specs=tok_spec,
        out_shape=jax.ShapeDtypeStruct((batch, seq, D_MODEL), BF16),
        scratch_shapes=[pltpu.VMEM((3, 2 * NA_ROWS * GRID_W, NA_BAND * GRID_W), F32),
                        pltpu.VMEM((nb, seq, 2 * LANES), BF16)],
        compiler_params=_params(("arbitrary", "arbitrary")),
        name="na_attention",
    )(q3, k3, v3, ck, cv, rpb_pad)
    return out.reshape(batch * seq, D_MODEL)


def _gqa_attn_kernel(*refs, seq, windowed):
    if windowed:
        sink_ref, q_ref, k_ref, v_ref, kc_ref, vc_ref, o_ref, v4_scr, bias_scr = refs
    else:
        q_ref, k_ref, v_ref, kc_ref, vc_ref, o_ref, v4_scr = refs
    kvh = pl.program_id(1)
    low = _lane_is_low()
    high = jnp.logical_not(low)
    band = GQA_BAND
    stacked = GQA_GROUP * Q_BLOCK
    n_blocks = seq // Q_BLOCK

    head_order = list(range(0, GQA_GROUP, 2)) + list(range(1, GQA_GROUP, 2))

    kct = kc_ref[...].astype(BF16)
    vct = vc_ref[...].astype(BF16)
    kcb = jnp.concatenate([kct, kct], axis=0)
    vc4 = _values_with_ones_t(vct, vct)
    v4_scr[...] = _values_with_ones(v_ref[...], low)

    if windowed:
        q_shift = Q_BLOCK.bit_length() - 1
        block_of_row = lax.shift_right_logical(lax.broadcasted_iota(jnp.int32, (stacked, 1), 0), q_shift)
        sink_col = jnp.zeros((stacked, 1), F32)
        for blk_i, g in enumerate(head_order):
            sink_col = jnp.where(block_of_row == blk_i, sink_ref[kvh * GQA_GROUP + g] * LOG2E, sink_col)

        @pl.when((pl.program_id(0) == 0) & (kvh == 0))
        def _():
            q_off = lax.broadcasted_iota(jnp.int32, (stacked, band), 0) & (Q_BLOCK - 1)
            k_off = lax.broadcasted_iota(jnp.int32, (stacked, band), 1)
            for t in range(bias_scr.shape[0]):
                dist = q_off - k_off + t * Q_BLOCK
                bias_scr[t] = jnp.where(jnp.abs(dist) <= WINDOW, 0.0, NEG_INF)

    def key_rows(i):
        if not windowed:
            return slice(None)
        ks = int(np.clip(i * Q_BLOCK - WINDOW, 0, seq - band))
        return slice(ks, ks + band)

    def scores_of(i):
        r0 = i * Q_BLOCK
        qblk = q_ref[r0:r0 + Q_BLOCK, :]
        zero = jnp.zeros((Q_BLOCK, LANES), qblk.dtype)
        parts = []
        for g in head_order:
            blk = qblk[:, (g // 2) * LANES:(g // 2 + 1) * LANES]
            parts.append(jnp.where(low if g % 2 == 0 else high, blk, zero))
        qs = jnp.concatenate(parts, axis=0)
        keys = key_rows(i)
        s_lat = _dot_nt(qs, k_ref[keys, :])
        if windowed:
            s_lat = s_lat + bias_scr[(r0 - keys.start) // Q_BLOCK]
        return [s_lat, _dot(qs, kcb)]

    nxt = scores_of(0)
    for i in range(n_blocks):
        cur, nxt = nxt, (scores_of(i + 1) if i + 1 < n_blocks else None)
        ps, extra = _softmax_numerators(cur, sink_col if windowed else None)
        even, odd = _stacked_pv(ps, [v4_scr[key_rows(i), :], vc4], low, extra, transposed=(False, True))
        r0 = i * Q_BLOCK
        for c in range(GQA_GROUP // 2):
            rows_c = slice(c * Q_BLOCK, (c + 1) * Q_BLOCK)
            o_ref[r0:r0 + Q_BLOCK, c * LANES:(c + 1) * LANES] = jnp.where(low, even[rows_c], odd[rows_c]).astype(o_ref.dtype)


def _gqa_attention(q, kd, vd, cache_k, cache_v, *, batch, seq, layer_slot, sink=None):
    group_w = GQA_GROUP * HEAD_DIM
    q3 = q.reshape(batch, seq, D_MODEL)
    kd3 = kd.reshape(batch, seq, N_KV_GQA * LANES)
    vd3 = vd.reshape(batch, seq, N_KV_GQA * LANES)
    past = cache_k.shape[2]
    ck, cv = (_cache_transposed(a) for a in (cache_k, cache_v))
    q_spec = pl.BlockSpec((None, seq, group_w), lambda b, h: (b, 0, h))
    kv_spec = pl.BlockSpec((None, seq, LANES), lambda b, h: (b, 0, h))
    ctx_spec = pl.BlockSpec((None, None, HEAD_DIM, past), lambda b, h: (b, layer_slot, h, 0))
    in_specs = [q_spec, kv_spec, kv_spec, ctx_spec, ctx_spec]
    args = [q3, kd3, vd3, ck, cv]
    scratch = [pltpu.VMEM((seq, 2 * LANES), BF16)]
    if sink is not None:
        in_specs.insert(0, pl.BlockSpec(memory_space=pltpu.SMEM))
        args.insert(0, sink)
        n_offsets = (GQA_BAND - Q_BLOCK) // Q_BLOCK + 1
        scratch.append(pltpu.VMEM((n_offsets, GQA_GROUP * Q_BLOCK, GQA_BAND), F32))
    out = pl.pallas_call(
        functools.partial(_gqa_attn_kernel, seq=seq, windowed=sink is not None),
        grid=(batch, N_KV_GQA),
        in_specs=in_specs,
        out_specs=q_spec,
        out_shape=jax.ShapeDtypeStruct((batch, seq, D_MODEL), BF16),
        scratch_shapes=scratch,
        compiler_params=_params(("arbitrary", "arbitrary")),
        name="gqa_attention",
    )(*args)
    return out.reshape(batch * seq, D_MODEL)


def _rope_tables(seq):
    t = jnp.arange(seq)
    rows = (t // GRID_W).astype(F32)
    cols = (t % GRID_W).astype(F32)
    quarter = HEAD_DIM // 4
    freqs = jnp.exp(-math.log(ROPE_BASE) * jnp.arange(quarter, dtype=F32) / quarter)
    ang_r = rows[:, None] * freqs[None, :]
    ang_c = cols[:, None] * freqs[None, :]
    cos = jnp.concatenate([jnp.cos(ang_r)] * 2 + [jnp.cos(ang_c)] * 2, axis=1)
    sin = jnp.concatenate([-jnp.sin(ang_r), jnp.sin(ang_r), -jnp.sin(ang_c), jnp.sin(ang_c)], axis=1)
    return jnp.tile(cos, (1, LANES // HEAD_DIM)), jnp.tile(sin, (1, LANES // HEAD_DIM))


def kernel(x_prompt, x_sample, cache_k_a, cache_v_a, cache_k_b, cache_v_b, cache_k_c, cache_v_c, c, c_ctx,
           ada_w, ada_b, norm_mix_g, norm_mlp_g, w_o, mlp_w1, mlp_b1, mlp_w2, mlp_b2, w_qkv_a, rpb_a,
           w_qkv_b, sink_b, w_qkv_c, q_norm_c, k_norm_c, final_norm_g):
    batch, seq, _ = x_prompt.shape
    dec_batch, dec_seq, _ = x_sample.shape
    depth = ada_w.shape[0]
    assert dec_batch + 1 <= COND_ROWS and dec_seq % GRID_W == 0
    assert (dec_seq // GRID_W) % NA_ROWS == 0 and dec_seq // GRID_W >= 3 * NA_ROWS

    cond = jnp.concatenate([c, c_ctx[None, :], jnp.zeros((COND_ROWS - dec_batch - 1, D_MODEL), F32)], axis=0)
    mods = _ada_mods(cond, ada_w, ada_b).reshape(depth, COND_ROWS, ADA_CHUNKS, D_MODEL)

    w_qkv = (w_qkv_a, w_qkv_b, w_qkv_c)
    n_kv = (N_KV_A, N_KV_GQA, N_KV_GQA)
    caches_k = (cache_k_a, cache_k_b, cache_k_c)
    caches_v = (cache_v_a, cache_v_b, cache_v_c)
    new_kv = [None] * N_MIXERS
    rope = _rope_tables(dec_seq)
    nq = N_HEADS * HEAD_DIM

    xp = x_prompt.reshape(batch * seq, D_MODEL)
    xs = x_sample.reshape(dec_batch * dec_seq, D_MODEL)

    def layer_weights_f32(l):
        m, j = l % N_MIXERS, l // N_MIXERS
        nk = n_kv[m] * HEAD_DIM
        return [(w_qkv[m], j, 0, nq), (w_qkv[m], j, nq, nk), (w_qkv[m], j, nq + nk, nk), (w_o, l, 0, D_MODEL),
                (mlp_w1, l, 0, D_FF), (mlp_w2, l, 0, D_MODEL)]

    first = layer_weights_f32(0)
    weights = [w[la, :, c0:c0 + nc].astype(BF16) for w, la, c0, nc in first[:3]] + [None] * 3

    for l in range(depth):
        m, j = l % N_MIXERS, l // N_MIXERS
        wq, wk, wv = weights[:3]
        b1 = mlp_b1[l][None, :]
        b2 = mlp_b2[l][None, :]
        g_mix = norm_mix_g[l][None, :]
        g_mlp = norm_mlp_g[l][None, :]
        mods_ctx = mods[l, dec_batch:dec_batch + 1]
        mods_lat = mods[l, :dec_batch]
        final = final_norm_g[None, :] if l == depth - 1 else None
        norm_args, ctx_norm_args = {}, {}
        if m == 2:
            qn = jnp.tile(q_norm_c[j], LANES // HEAD_DIM)[None, :]
            norm_args = dict(qn=qn, kn=jnp.tile(k_norm_c[j], LANES // HEAD_DIM)[None, :])
            ctx_norm_args = dict(qn=qn, kn_col=jnp.tile(k_norm_c[j], n_kv[m])[:, None])
        sink = sink_b[j] if m == 1 else None

        o_ctx, kt, vt = _ctx_front(xp, mods_ctx, g_mix, wq, wk.T, wv.T, n_kv=n_kv[m], seq=seq, prev=new_kv[m],
                                   sink=sink, **ctx_norm_args)
        new_kv[m] = (kt, vt)

        if m == 0:
            q, k, v, casted = _qkv_proj(xs, mods_lat, g_mix, wq, wk, wv, n_kv=n_kv[m], dup_kv=False,
                                        to_cast=first[3:] if l == 0 else ())
            if l == 0:
                weights[3:] = casted
            o_lat = _na_attention(q, k, v, caches_k[m], caches_v[m], _na_padded_rpb(rpb_a[j]),
                                  batch=dec_batch, seq=dec_seq, layer_slot=j)
        else:
            q, k, v, _ = _qkv_proj(xs, mods_lat, g_mix, wq, wk, wv, n_kv=n_kv[m], dup_kv=True,
                                   rope=rope, seq=dec_seq, **norm_args)
            o_lat = _gqa_attention(q, k, v, caches_k[m], caches_v[m], batch=dec_batch, seq=dec_seq, layer_slot=j,
                                   sink=sink)

        wo, w1, w2 = weights[3:]
        to_cast = layer_weights_f32(l + 1) if l + 1 < depth else ()
        xp, xs, weights = _mlp_block(xp, o_ctx, xs, o_lat, mods[l, :dec_batch + 1], wo, g_mlp, w1, b1, w2, b2, final,
                                     to_cast=to_cast)

    y_prompt = xp.reshape(batch, seq, D_MODEL)
    y_sample = xs.reshape(dec_batch, dec_seq, D_MODEL)
    outs = [y_prompt, y_sample]
    for m in range(N_MIXERS):
        for stacked in new_kv[m]:
            t = stacked.reshape(batch, stacked.shape[1], n_kv[m], HEAD_DIM, seq)
            outs.append(jnp.transpose(t, (0, 1, 4, 2, 3)))
    return tuple(outs)
```

```python
import functools
import math

import numpy as np
import jax
import jax.numpy as jnp
from jax import lax
from jax.experimental import pallas as pl
from jax.experimental.pallas import tpu as pltpu

D_MODEL = 1024
N_HEADS = 16
HEAD_DIM = 64
N_KV_A = 16
N_KV_GQA = 4
GQA_GROUP = N_HEADS // N_KV_GQA
D_FF = 4 * D_MODEL
ADA_CHUNKS = 6
N_MIXERS = 3
GRID_W = 64
WINDOW = 128
WIN_H = 8
WIN_W = 16
ROPE_BASE = 10000.0
EPS = 1e-6
NEG_INF = -1e30
LOG2E = math.log2(math.e)
Q_SCALE = HEAD_DIM ** -0.5 * LOG2E

HEAD_SHIFT = HEAD_DIM.bit_length() - 1
LANES = 128
COND_ROWS = 16
ADA_BLOCK = 3 * D_MODEL
TOKEN_TILE = 512
MLP_CHUNK = 1024
QKV_TILE = 1024
QKV_SUB_TILE = 512
NA_ROWS = 4
NA_BAND = NA_ROWS + WIN_H
NA_BATCH = 2
Q_BLOCK = 128
GQA_BAND = Q_BLOCK + 2 * WINDOW
VMEM_LIMIT = 56 * 1024 * 1024

BF16 = jnp.bfloat16
F32 = jnp.float32


def _dot(a, b):
    return jnp.dot(a, b, preferred_element_type=F32)


def _dot_nt(a, b):
    return lax.dot_general(a, b, (((1,), (1,)), ((), ())), preferred_element_type=F32)


def _rmsnorm_rows(x, g):
    return x * lax.rsqrt(jnp.mean(x * x, axis=-1, keepdims=True) + EPS) * g


def _lane_is_low(shape=(1, LANES)):
    return lax.broadcasted_iota(jnp.int32, shape, len(shape) - 1) < HEAD_DIM


def _const_spec(shape):
    return pl.BlockSpec(shape, lambda *_: (0,) * len(shape), pipeline_mode=pl.Buffered(1))


def _params(semantics):
    return pltpu.CompilerParams(dimension_semantics=semantics, vmem_limit_bytes=VMEM_LIMIT)


def _cast_operands(to_cast, steps, step_of):
    in_specs, args, out_specs, out_shape = [], [], [], []
    for w, layer, col0, ncols in to_cast:
        n_layers, rows, cols = w.shape
        slab = rows // steps
        assert rows % steps == 0 and slab % 16 == 0 and col0 % ncols == 0
        in_specs.append(pl.BlockSpec((None, None, slab, ncols),
                                     lambda i, la=layer, cb=col0 // ncols: (la, step_of(i), 0, cb)))
        args.append(w.reshape(n_layers, steps, slab, cols))
        out_specs.append(pl.BlockSpec((None, slab, ncols), lambda i: (step_of(i), 0, 0)))
        out_shape.append(jax.ShapeDtypeStruct((steps, slab, ncols), BF16))
    return in_specs, args, out_specs, out_shape


def _ada_kernel(c_ref, w_ref, b_ref, o_ref):
    c = c_ref[...]
    s = (c / (1.0 + jnp.exp(-c))).astype(BF16)
    o_ref[...] = _dot(s, w_ref[...].astype(BF16)) + b_ref[...]


def _ada_mods(cond, ada_w, ada_b):
    depth, _, n_out = ada_w.shape
    nb = n_out // ADA_BLOCK
    return pl.pallas_call(
        _ada_kernel,
        grid=(depth, nb),
        in_specs=[
            pl.BlockSpec((COND_ROWS, D_MODEL), lambda l, n: (0, 0)),
            pl.BlockSpec((None, D_MODEL, ADA_BLOCK), lambda l, n: (l, 0, n)),
            pl.BlockSpec((None, 1, ADA_BLOCK), lambda l, n: (l, 0, n)),
        ],
        out_specs=pl.BlockSpec((None, COND_ROWS, ADA_BLOCK), lambda l, n: (l, 0, n)),
        out_shape=jax.ShapeDtypeStruct((depth, COND_ROWS, n_out), F32),
        compiler_params=_params(("arbitrary", "arbitrary")),
        name="ada_mods",
    )(cond, ada_w, ada_b.reshape(depth, 1, n_out))


def _head_rmsnorm(blk, gain):
    sq = blk * blk
    hi = sq.astype(BF16)
    lo = (sq - hi.astype(F32)).astype(BF16)
    r = lax.shift_right_logical(lax.broadcasted_iota(jnp.int32, (LANES, LANES), 0), HEAD_SHIFT)
    c = lax.shift_right_logical(lax.broadcasted_iota(jnp.int32, (LANES, LANES), 1), HEAD_SHIFT)
    avg = jnp.where(r == c, 1.0 / HEAD_DIM, 0.0).astype(BF16)
    ms = _dot(hi, avg) + _dot(lo, avg)
    return blk * lax.rsqrt(ms + EPS) * gain


def _rope_block(blk, cos, sin_signed):
    lane = lax.broadcasted_iota(jnp.int32, (1, LANES), 1)
    first = (lane & 16) == 0
    partner = jnp.where(first, pltpu.roll(blk, LANES - 16, 1), pltpu.roll(blk, 16, 1))
    return blk * cos + partner * sin_signed


def _qkv_kernel(*refs, n_kv, qk_norm, rope, dup_kv, n_cast):
    x_ref, mods_ref, g_ref, wq_ref, wk_ref, wv_ref = refs[:6]
    pos = 6
    if qk_norm:
        qn_ref, kn_ref = refs[pos:pos + 2]
        pos += 2
    if rope:
        cos_ref, sin_ref = refs[pos:pos + 2]
        pos += 2
    cast_in = refs[pos:pos + n_cast]
    pos += n_cast
    q_ref, k_ref, v_ref = refs[pos:pos + 3]
    cast_out = refs[pos + 3:]
    low = _lane_is_low()
    n_sub = x_ref.shape[0] // QKV_SUB_TILE

    def project(i):
        rows = slice(i * QKV_SUB_TILE, (i + 1) * QKV_SUB_TILE)
        h = _rmsnorm_rows(x_ref[rows, :], g_ref[...]) * (1.0 + mods_ref[1:2, :]) + mods_ref[0:1, :]
        h = h.astype(BF16)
        return _dot(h, wq_ref[...]), _dot(h, wk_ref[...]), _dot(h, wv_ref[...])

    def finish(i, qkv):
        q, k, v = qkv
        rows = slice(i * QKV_SUB_TILE, (i + 1) * QKV_SUB_TILE)

        def post(blk, gain_ref):
            if qk_norm:
                blk = _head_rmsnorm(blk, gain_ref[...])
            if rope:
                blk = _rope_block(blk, cos_ref[rows, :], sin_ref[rows, :])
            return blk

        for cb in range(N_HEADS * HEAD_DIM // LANES):
            sl = slice(cb * LANES, (cb + 1) * LANES)
            blk = post(q[:, sl], qn_ref if qk_norm else None)
            q_ref[rows, sl] = (blk * Q_SCALE).astype(q_ref.dtype)

        for cb in range(n_kv * HEAD_DIM // LANES):
            sl = slice(cb * LANES, (cb + 1) * LANES)
            kb = post(k[:, sl], kn_ref if qk_norm else None)
            vb = v[:, sl]
            if dup_kv:
                for arr, ref in ((kb, k_ref), (vb, v_ref)):
                    rolled = pltpu.roll(arr, HEAD_DIM, 1)
                    ref[rows, (2 * cb) * LANES:(2 * cb + 1) * LANES] = jnp.where(low, arr, rolled).astype(ref.dtype)
                    ref[rows, (2 * cb + 1) * LANES:(2 * cb + 2) * LANES] = jnp.where(low, rolled, arr).astype(ref.dtype)
            else:
                k_ref[rows, sl] = kb.astype(k_ref.dtype)
                v_ref[rows, sl] = vb.astype(v_ref.dtype)

    nxt = project(0)
    for i in range(n_sub):
        cur, nxt = nxt, (project(i + 1) if i + 1 < n_sub else None)
        finish(i, cur)
    for src, dst in zip(cast_in, cast_out):
        dst[...] = src[...].astype(dst.dtype)


def _ctx_front_kernel(*refs, n_kv, qk_norm, seq, n_prev, has_sink):
    if has_sink:
        sink_ref, refs = refs[0], refs[1:]
    x_ref, mods_ref, g_ref, wq_ref, wkt_ref, wvt_ref = refs[:6]
    pos = 6
    if qk_norm:
        qn_ref, kn_ref = refs[pos:pos + 2]
        pos += 2
    if n_prev:
        kprev_ref, vprev_ref = refs[pos:pos + 2]
    o_ref, kt_ref, vt_ref = refs[-3:]
    low = _lane_is_low()
    n_pairs = N_HEADS // 2
    n_sub = x_ref.shape[0] // seq
    ones = jnp.ones((HEAD_DIM, seq), BF16)

    if n_prev:
        kt_ref[:, :n_prev] = kprev_ref[...]
        vt_ref[:, :n_prev] = vprev_ref[...]

    def head_rows(h):
        kvh = h * n_kv // N_HEADS
        return slice(kvh * HEAD_DIM, (kvh + 1) * HEAD_DIM)

    def project(b):
        rows = slice(b * seq, (b + 1) * seq)
        h = _rmsnorm_rows(x_ref[rows, :], g_ref[...]) * (1.0 + mods_ref[1:2, :]) + mods_ref[0:1, :]
        h = h.astype(BF16)
        q = _dot(h, wq_ref[...])
        q_pairs = []
        for cb in range(n_pairs):
            blk = q[:, cb * LANES:(cb + 1) * LANES]
            if qk_norm:
                blk = _head_rmsnorm(blk, qn_ref[...])
            q_pairs.append((blk * Q_SCALE).astype(BF16))
        kt = _dot_nt(wkt_ref[...], h)
        vt = _dot_nt(wvt_ref[...], h)
        if qk_norm:
            k3 = kt.reshape(n_kv, HEAD_DIM, seq)
            k3 = k3 * lax.rsqrt(jnp.mean(k3 * k3, axis=1, keepdims=True) + EPS)
            kt = k3.reshape(kt.shape) * kn_ref[...]
        kt_ref[b, n_prev] = kt
        vt_ref[b, n_prev] = vt
        return q_pairs, kt.astype(BF16), vt.astype(BF16)

    def attend(b, projected):
        q_pairs, kt, vt = projected
        rows = slice(b * seq, (b + 1) * seq)

        def scores_of(p):
            ktp = jnp.concatenate([kt[head_rows(2 * p), :], kt[head_rows(2 * p + 1), :]], axis=0)
            return [_dot(_stack_heads(q_pairs[p], low), ktp)]

        nxt = scores_of(0)
        for p in range(n_pairs):
            cur, nxt = nxt, (scores_of(p + 1) if p + 1 < n_pairs else None)
            sink_col = None
            if has_sink:
                first_head = lax.broadcasted_iota(jnp.int32, (2 * seq, 1), 0) < seq
                sink_col = jnp.where(first_head, sink_ref[2 * p], sink_ref[2 * p + 1]) * LOG2E
            ps, extra = _softmax_numerators(cur, sink_col)
            v4t = jnp.concatenate([vt[head_rows(2 * p), :], ones, ones, vt[head_rows(2 * p + 1), :]], axis=0)
            even, odd = _stacked_pv(ps, [v4t], low, extra, transposed=(True,))
            o_ref[rows, p * LANES:(p + 1) * LANES] = jnp.where(low, even, odd).astype(o_ref.dtype)

    nxt = project(0)
    for b in range(n_sub):
        cur, nxt = nxt, (project(b + 1) if b + 1 < n_sub else None)
        attend(b, cur)


def _ctx_front(x, mods, gain, wq, wkt, wvt, *, n_kv, seq, qn=None, kn_col=None, prev=None, sink=None):
    tokens = x.shape[0]
    tm = TOKEN_TILE
    steps = tokens // tm
    nk = n_kv * HEAD_DIM
    n_prev = 0 if prev is None else prev[0].shape[1]
    in_specs = [
        pl.BlockSpec((tm, D_MODEL), lambda i: (i, 0)),
        pl.BlockSpec((None, ADA_CHUNKS, D_MODEL), lambda i: (0, 0, 0)),
        _const_spec((1, D_MODEL)),
        _const_spec((D_MODEL, D_MODEL)),
        _const_spec((nk, D_MODEL)),
        _const_spec((nk, D_MODEL)),
    ]
    args = [x, mods, gain, wq, wkt, wvt]
    if qn is not None:
        in_specs += [_const_spec((1, LANES)), _const_spec((nk, 1))]
        args += [qn, kn_col]
    if n_prev:
        in_specs += [pl.BlockSpec((tm // seq, n_prev, nk, seq), lambda i: (i, 0, 0, 0))] * 2
        args += list(prev)
    if sink is not None:
        in_specs.insert(0, pl.BlockSpec(memory_space=pltpu.SMEM))
        args.insert(0, sink)
    kv_spec = pl.BlockSpec((tm // seq, n_prev + 1, nk, seq), lambda i: (i, 0, 0, 0))
    kv_shape = jax.ShapeDtypeStruct((tokens // seq, n_prev + 1, nk, seq), F32)
    return pl.pallas_call(
        functools.partial(_ctx_front_kernel, n_kv=n_kv, qk_norm=qn is not None, seq=seq, n_prev=n_prev,
                          has_sink=sink is not None),
        grid=(steps,),
        in_specs=in_specs,
        out_specs=[pl.BlockSpec((tm, D_MODEL), lambda i: (i, 0)), kv_spec, kv_spec],
        out_shape=[jax.ShapeDtypeStruct((tokens, D_MODEL), BF16), kv_shape, kv_shape],
        compiler_params=_params(("arbitrary",)),
        name="ctx_front",
    )(*args)


def _qkv_proj(x, mods, gain, wq, wk, wv, *, n_kv, dup_kv, qn=None, kn=None, rope=None, seq=None, to_cast=()):
    tokens = x.shape[0]
    tm = QKV_TILE
    steps = tokens // tm
    tiles_per_cond = steps // mods.shape[0]
    nk = n_kv * HEAD_DIM
    nk_out = 2 * nk if dup_kv else nk
    in_specs = [
        pl.BlockSpec((tm, D_MODEL), lambda i: (i, 0)),
        pl.BlockSpec((None, ADA_CHUNKS, D_MODEL), lambda i: (i // tiles_per_cond, 0, 0)),
        _const_spec((1, D_MODEL)),
        _const_spec((D_MODEL, D_MODEL)),
        _const_spec((D_MODEL, nk)),
        _const_spec((D_MODEL, nk)),
    ]
    args = [x, mods, gain, wq, wk, wv]
    if qn is not None:
        in_specs += [_const_spec((1, LANES)), _const_spec((1, LANES))]
        args += [qn, kn]
    if rope is not None:
        tiles_per_seq = seq // tm
        in_specs += [pl.BlockSpec((tm, LANES), lambda i: (i % tiles_per_seq, 0))] * 2
        args += list(rope)
    cast_in, cast_args, cast_out, cast_shape = _cast_operands(to_cast, steps, lambda i: i)
    outs = pl.pallas_call(
        functools.partial(_qkv_kernel, n_kv=n_kv, qk_norm=qn is not None, rope=rope is not None, dup_kv=dup_kv,
                          n_cast=len(to_cast)),
        grid=(steps,),
        in_specs=in_specs + cast_in,
        out_specs=[
            pl.BlockSpec((tm, D_MODEL), lambda i: (i, 0)),
            pl.BlockSpec((tm, nk_out), lambda i: (i, 0)),
            pl.BlockSpec((tm, nk_out), lambda i: (i, 0)),
        ] + cast_out,
        out_shape=[
            jax.ShapeDtypeStruct((tokens, D_MODEL), BF16),
            jax.ShapeDtypeStruct((tokens, nk_out), BF16),
            jax.ShapeDtypeStruct((tokens, nk_out), BF16),
        ] + cast_shape,
        compiler_params=_params(("arbitrary",)),
        name="qkv_proj",
    )(*args, *cast_args)
    casted = [c.reshape(w.shape[1], ncols) for c, (w, _, _, ncols) in zip(outs[3:], to_cast)]
    return outs[0], outs[1], outs[2], casted


def _mlp_kernel(*refs, final, n_cast, ctx_steps):
    xp_ref, op_ref, xs_ref, os_ref, mods_ref, wo_ref, g_ref, w1_ref, b1_ref, w2_ref, b2_ref = refs[:11]
    pos = 11
    fg_ref = refs[pos] if final else None
    pos += int(final)
    cast_in = refs[pos:pos + n_cast]
    outp_ref, outs_ref = refs[pos + n_cast:pos + n_cast + 2]
    cast_out = refs[pos + n_cast + 2:]

    def update(x_ref, o_ref, out_ref):
        x1 = x_ref[...] + mods_ref[2:3, :] * _dot(o_ref[...], wo_ref[...])
        h = _rmsnorm_rows(x1, g_ref[...]) * (1.0 + mods_ref[4:5, :]) + mods_ref[3:4, :]
        h = h.astype(BF16)
        acc = jnp.zeros_like(x1)
        for c in range(D_FF // MLP_CHUNK):
            sl = slice(c * MLP_CHUNK, (c + 1) * MLP_CHUNK)
            t = jnp.maximum(_dot(h, w1_ref[:, sl]) + b1_ref[:, sl], 0.0)
            acc = acc + _dot((t * t).astype(BF16), w2_ref[sl, :])
        x2 = x1 + mods_ref[5:6, :] * (acc + b2_ref[...])
        if final:
            x2 = _rmsnorm_rows(x2, fg_ref[...])
        out_ref[...] = x2

    @pl.when(pl.program_id(0) < ctx_steps)
    def _():
        update(xp_ref, op_ref, outp_ref)

    @pl.when(pl.program_id(0) >= ctx_steps)
    def _():
        update(xs_ref, os_ref, outs_ref)
        for src, dst in zip(cast_in, cast_out):
            dst[...] = src[...].astype(dst.dtype)


def _mlp_block(xp, op, xs, os_, mods, wo, gain, w1, b1, w2, b2, final_gain=None, to_cast=()):
    tm = TOKEN_TILE
    ctx_steps = xp.shape[0] // tm
    lat_steps = xs.shape[0] // tm
    n_cond = mods.shape[0] - 1
    tiles_per_cond = lat_steps // n_cond

    def ctx_tile(i):
        return jnp.minimum(i, ctx_steps - 1)

    def lat_tile(i):
        return jnp.maximum(i - ctx_steps, 0)

    in_specs = [
        pl.BlockSpec((tm, D_MODEL), lambda i: (ctx_tile(i), 0)),
        pl.BlockSpec((tm, D_MODEL), lambda i: (ctx_tile(i), 0)),
        pl.BlockSpec((tm, D_MODEL), lambda i: (lat_tile(i), 0)),
        pl.BlockSpec((tm, D_MODEL), lambda i: (lat_tile(i), 0)),
        pl.BlockSpec((None, ADA_CHUNKS, D_MODEL),
                     lambda i: (jnp.where(i < ctx_steps, n_cond, lat_tile(i) // tiles_per_cond), 0, 0)),
        _const_spec((D_MODEL, D_MODEL)),
        _const_spec((1, D_MODEL)),
        _const_spec((D_MODEL, D_FF)),
        _const_spec((1, D_FF)),
        _const_spec((D_FF, D_MODEL)),
        _const_spec((1, D_MODEL)),
    ]
    args = [xp, op, xs, os_, mods, wo, gain, w1, b1, w2, b2]
    if final_gain is not None:
        in_specs.append(_const_spec((1, D_MODEL)))
        args.append(final_gain)
    out_specs = [pl.BlockSpec((tm, D_MODEL), lambda i: (ctx_tile(i), 0)),
                 pl.BlockSpec((tm, D_MODEL), lambda i: (lat_tile(i), 0))]
    out_shape = [jax.ShapeDtypeStruct(xp.shape, F32), jax.ShapeDtypeStruct(xs.shape, F32)]
    cast_in, cast_args, cast_out, cast_shape = _cast_operands(to_cast, lat_steps, lat_tile)
    in_specs += cast_in
    args += cast_args
    out_specs += cast_out
    out_shape += cast_shape
    outs = pl.pallas_call(
        functools.partial(_mlp_kernel, final=final_gain is not None, n_cast=len(to_cast), ctx_steps=ctx_steps),
        grid=(ctx_steps + lat_steps,),
        in_specs=in_specs,
        out_specs=out_specs,
        out_shape=out_shape,
        compiler_params=_params(("arbitrary",)),
        name="mlp_block",
    )(*args)
    casted = [c.reshape(w.shape[1], ncols) for c, (w, _, _, ncols) in zip(outs[2:], to_cast)]
    return outs[0], outs[1], casted


def _softmax_numerators(scores, extra_logit=None):
    m = functools.reduce(jnp.maximum, [jnp.max(s, axis=-1, keepdims=True) for s in scores])
    if extra_logit is not None:
        m = jnp.maximum(m, extra_logit)
    ps = [jnp.exp2(s - m).astype(BF16) for s in scores]
    return ps, (None if extra_logit is None else jnp.exp2(extra_logit - m))


def _with_ones(vals, value_lanes):
    return jnp.where(value_lanes, vals, jnp.ones_like(vals))


def _stack_heads(qb, low):
    zero = jnp.zeros_like(qb)
    return jnp.concatenate([jnp.where(low, qb, zero), jnp.where(jnp.logical_not(low), qb, zero)], axis=0)


def _values_with_ones(vb, low):
    return jnp.concatenate([_with_ones(vb, low), _with_ones(vb, jnp.logical_not(low))], axis=1)


def _values_with_ones_t(vt_even, vt_odd):
    ones = jnp.ones_like(vt_even)
    return jnp.concatenate([vt_even, ones, ones, vt_odd], axis=0)


def _stacked_pv(ps, v4s, low, extra=None, transposed=None):
    high = jnp.logical_not(low)
    transposed = transposed or (False,) * len(ps)
    acc = functools.reduce(jnp.add, [(_dot_nt if t else _dot)(p, v) for p, v, t in zip(ps, v4s, transposed)])
    half = acc.shape[0] // 2
    outs = []
    for rows, lanes, ones_lanes in ((slice(0, half), slice(0, LANES), high), (slice(half, None), slice(LANES, None), low)):
        part = acc[rows, lanes]
        if extra is not None:
            part = part + jnp.where(ones_lanes, extra[rows], 0.0)
        denom = jnp.max(jnp.where(ones_lanes, part, 0.0), axis=-1, keepdims=True)
        outs.append(part * (1.0 / denom))
    return outs


def _na_group_geometry(rows):
    kh = min(WIN_H, rows)
    n_groups = rows // NA_ROWS
    masked = 2 * WIN_H - 1
    band0 = [int(np.clip(g * NA_ROWS - kh // 2, 0, rows - NA_BAND)) for g in range(n_groups)]
    table_of = [0 if g == 0 else (2 if g == n_groups - 1 else 1) for g in range(n_groups)]
    sel = np.full((3, NA_ROWS, NA_BAND), masked, np.int64)
    for g in (0, 1, n_groups - 1):
        for a in range(NA_ROWS):
            r = g * NA_ROWS + a
            r0 = int(np.clip(r - kh // 2, 0, rows - kh))
            for i in range(NA_BAND):
                kr = band0[g] + i
                if r0 <= kr < r0 + kh:
                    sel[table_of[g], a, i] = kr - r + WIN_H - 1
    return band0, table_of, sel


def _na_attn_kernel(q_ref, k_ref, v_ref, kc_ref, vc_ref, rpb_ref, o_ref, tab_scr, v4_scr, *, rows):
    low = _lane_is_low()
    band0, table_of, sel = _na_group_geometry(rows)
    q_rows = NA_ROWS * GRID_W
    k_rows = NA_BAND * GRID_W
    n_groups = rows // NA_ROWS

    @pl.when(pl.program_id(1) == 0)
    def _():
        qc = lax.broadcasted_iota(jnp.int32, (GRID_W, GRID_W), 0)
        kc = lax.broadcasted_iota(jnp.int32, (GRID_W, GRID_W), 1)
        c0 = jnp.clip(qc - WIN_W // 2, 0, GRID_W - WIN_W)
        col_valid = (kc >= c0) & (kc < c0 + WIN_W)
        for half in range(2):
            blocks = []
            for ri in range(2 * WIN_H - 1):
                row = jnp.broadcast_to(rpb_ref[half, ri:ri + 1, :], (GRID_W, 2 * GRID_W))
                skew = pltpu.roll(row, GRID_W + 1, 1, stride=1, stride_axis=0)
                blocks.append(jnp.where(col_valid, skew[:, :GRID_W], NEG_INF))
            blocks.append(jnp.full((GRID_W, GRID_W), NEG_INF, F32))
            for t in range(3):
                for a in range(NA_ROWS):
                    strip = [blocks[int(sel[t, a, i])] for i in range(NA_BAND)]
                    r = half * q_rows + a * GRID_W
                    tab_scr[t, r:r + GRID_W, :] = jnp.concatenate(strip, axis=1)

    n_batch = q_ref.shape[0]
    kcb = [kc_ref[bb].astype(BF16) for bb in range(n_batch)]
    vct = [vc_ref[bb].astype(BF16) for bb in range(n_batch)]
    vc4 = [_values_with_ones_t(v[:HEAD_DIM], v[HEAD_DIM:]) for v in vct]
    for bb in range(n_batch):
        v4_scr[bb] = _values_with_ones(v_ref[bb], low)

    def scores_of(item):
        bb, g = item
        rq = g * q_rows
        ks = band0[g] * GRID_W
        qs = _stack_heads(q_ref[bb, rq:rq + q_rows, :], low)
        return [_dot_nt(qs, k_ref[bb, ks:ks + k_rows, :]) + tab_scr[table_of[g]], _dot(qs, kcb[bb])]

    items = [(bb, g) for bb in range(n_batch) for g in range(n_groups)]
    nxt = scores_of(items[0])
    for idx, (bb, g) in enumerate(items):
        cur, nxt = nxt, (scores_of(items[idx + 1]) if idx + 1 < len(items) else None)
        ks = band0[g] * GRID_W
        ps, _ = _softmax_numerators(cur)
        lo, hi = _stacked_pv(ps, [v4_scr[bb, ks:ks + k_rows, :], vc4[bb]], low, transposed=(False, True))
        o_ref[bb, g * q_rows:(g + 1) * q_rows, :] = jnp.where(low, lo, hi).astype(o_ref.dtype)


def _na_padded_rpb(rpb):
    pad_l = GRID_W - WIN_W
    return jnp.pad(rpb * LOG2E, ((0, 0), (0, 1), (pad_l, 2 * GRID_W - pad_l - rpb.shape[2])))


def _cache_transposed(cache):
    b, n, past, h, d = cache.shape
    return jnp.transpose(cache, (0, 1, 3, 4, 2)).reshape(b, n, h * d, past)


def _na_attention(q, k, v, cache_k, cache_v, rpb_pad, *, batch, seq, layer_slot):
    rows = seq // GRID_W
    n_pairs = N_HEADS // 2
    q3, k3, v3 = (a.reshape(batch, seq, D_MODEL) for a in (q, k, v))
    past = cache_k.shape[2]
    ck, cv = (_cache_transposed(a) for a in (cache_k, cache_v))
    nb = NA_BATCH
    tok_spec = pl.BlockSpec((nb, seq, LANES), lambda p, b: (b, 0, p))
    ctx_spec = pl.BlockSpec((nb, None, LANES, past), lambda p, b: (b, layer_slot, p, 0))
    out = pl.pallas_call(
        functools.partial(_na_attn_kernel, rows=rows),
        grid=(n_pairs, batch // nb),
        in_specs=[tok_spec, tok_spec, tok_spec, ctx_spec, ctx_spec,
                  pl.BlockSpec((2,) + rpb_pad.shape[1:], lambda p, b: (p, 0, 0))],
        out_specs=tok_spec,
        out_shape=jax.ShapeDtypeStruct((batch, seq, D_MODEL), BF16),
        scratch_shapes=[pltpu.VMEM((3, 2 * NA_ROWS * GRID_W, NA_BAND * GRID_W), F32),
                        pltpu.VMEM((nb, seq, 2 * LANES), BF16)],
        compiler_params=_params(("arbitrary", "arbitrary")),
        name="na_attention",
    )(q3, k3, v3, ck, cv, rpb_pad)
    return out.reshape(batch * seq, D_MODEL)


def _gqa_attn_kernel(*refs, seq, windowed):
    if windowed:
        sink_ref, q_ref, k_ref, v_ref, kc_ref, vc_ref, o_ref, v4_scr, bias_scr = refs
    else:
        q_ref, k_ref, v_ref, kc_ref, vc_ref, o_ref, v4_scr = refs
    kvh = pl.program_id(1)
    low = _lane_is_low()
    high = jnp.logical_not(low)
    band = GQA_BAND
    stacked = GQA_GROUP * Q_BLOCK
    n_blocks = seq // Q_BLOCK

    head_order = list(range(0, GQA_GROUP, 2)) + list(range(1, GQA_GROUP, 2))

    kct = kc_ref[...].astype(BF16)
    vct = vc_ref[...].astype(BF16)
    kcb = jnp.concatenate([kct, kct], axis=0)
    vc4 = _values_with_ones_t(vct, vct)
    v4_scr[...] = _values_with_ones(v_ref[...], low)

    if windowed:
        q_shift = Q_BLOCK.bit_length() - 1
        block_of_row = lax.shift_right_logical(lax.broadcasted_iota(jnp.int32, (stacked, 1), 0), q_shift)
        sink_col = jnp.zeros((stacked, 1), F32)
        for blk_i, g in enumerate(head_order):
            sink_col = jnp.where(block_of_row == blk_i, sink_ref[kvh * GQA_GROUP + g] * LOG2E, sink_col)

        @pl.when((pl.program_id(0) == 0) & (kvh == 0))
        def _():
            q_off = lax.broadcasted_iota(jnp.int32, (stacked, band), 0) & (Q_BLOCK - 1)
            k_off = lax.broadcasted_iota(jnp.int32, (stacked, band), 1)
            for t in range(bias_scr.shape[0]):
                dist = q_off - k_off + t * Q_BLOCK
                bias_scr[t] = jnp.where(jnp.abs(dist) <= WINDOW, 0.0, NEG_INF)

    def key_rows(i):
        if not windowed:
            return slice(None)
        ks = int(np.clip(i * Q_BLOCK - WINDOW, 0, seq - band))
        return slice(ks, ks + band)

    def scores_of(i):
        r0 = i * Q_BLOCK
        qblk = q_ref[r0:r0 + Q_BLOCK, :]
        zero = jnp.zeros((Q_BLOCK, LANES), qblk.dtype)
        parts = []
        for g in head_order:
            blk = qblk[:, (g // 2) * LANES:(g // 2 + 1) * LANES]
            parts.append(jnp.where(low if g % 2 == 0 else high, blk, zero))
        qs = jnp.concatenate(parts, axis=0)
        keys = key_rows(i)
        s_lat = _dot_nt(qs, k_ref[keys, :])
        if windowed:
            s_lat = s_lat + bias_scr[(r0 - keys.start) // Q_BLOCK]
        return [s_lat, _dot(qs, kcb)]

    nxt = scores_of(0)
    for i in range(n_blocks):
        cur, nxt = nxt, (scores_of(i + 1) if i + 1 < n_blocks else None)
        ps, extra = _softmax_numerators(cur, sink_col if windowed else None)
        even, odd = _stacked_pv(ps, [v4_scr[key_rows(i), :], vc4], low, extra, transposed=(False, True))
        r0 = i * Q_BLOCK
        for c in range(GQA_GROUP // 2):
            rows_c = slice(c * Q_BLOCK, (c + 1) * Q_BLOCK)
            o_ref[r0:r0 + Q_BLOCK, c * LANES:(c + 1) * LANES] = jnp.where(low, even[rows_c], odd[rows_c]).astype(o_ref.dtype)


def _gqa_attention(q, kd, vd, cache_k, cache_v, *, batch, seq, layer_slot, sink=None):
    group_w = GQA_GROUP * HEAD_DIM
    q3 = q.reshape(batch, seq, D_MODEL)
    kd3 = kd.reshape(batch, seq, N_KV_GQA * LANES)
    vd3 = vd.reshape(batch, seq, N_KV_GQA * LANES)
    past = cache_k.shape[2]
    ck, cv = (_cache_transposed(a) for a in (cache_k, cache_v))
    q_spec = pl.BlockSpec((None, seq, group_w), lambda b, h: (b, 0, h))
    kv_spec = pl.BlockSpec((None, seq, LANES), lambda b, h: (b, 0, h))
    ctx_spec = pl.BlockSpec((None, None, HEAD_DIM, past), lambda b, h: (b, layer_slot, h, 0))
    in_specs = [q_spec, kv_spec, kv_spec, ctx_spec, ctx_spec]
    args = [q3, kd3, vd3, ck, cv]
    scratch = [pltpu.VMEM((seq, 2 * LANES), BF16)]
    if sink is not None:
        in_specs.insert(0, pl.BlockSpec(memory_space=pltpu.SMEM))
        args.insert(0, sink)
        n_offsets = (GQA_BAND - Q_BLOCK) // Q_BLOCK + 1
        scratch.append(pltpu.VMEM((n_offsets, GQA_GROUP * Q_BLOCK, GQA_BAND), F32))
    out = pl.pallas_call(
        functools.partial(_gqa_attn_kernel, seq=seq, windowed=sink is not None),
        grid=(batch, N_KV_GQA),
        in_specs=in_specs,
        out_specs=q_spec,
        out_shape=jax.ShapeDtypeStruct((batch, seq, D_MODEL), BF16),
        scratch_shapes=scratch,
        compiler_params=_params(("arbitrary", "arbitrary")),
        name="gqa_attention",
    )(*args)
    return out.reshape(batch * seq, D_MODEL)


def _rope_tables(seq):
    t = jnp.arange(seq)
    rows = (t // GRID_W).astype(F32)
    cols = (t % GRID_W).astype(F32)
    quarter = HEAD_DIM // 4
    freqs = jnp.exp(-math.log(ROPE_BASE) * jnp.arange(quarter, dtype=F32) / quarter)
    ang_r = rows[:, None] * freqs[None, :]
    ang_c = cols[:, None] * freqs[None, :]
    cos = jnp.concatenate([jnp.cos(ang_r)] * 2 + [jnp.cos(ang_c)] * 2, axis=1)
    sin = jnp.concatenate([-jnp.sin(ang_r), jnp.sin(ang_r), -jnp.sin(ang_c), jnp.sin(ang_c)], axis=1)
    return jnp.tile(cos, (1, LANES // HEAD_DIM)), jnp.tile(sin, (1, LANES // HEAD_DIM))


def kernel(x_prompt, x_sample, cache_k_a, cache_v_a, cache_k_b, cache_v_b, cache_k_c, cache_v_c, c, c_ctx,
           ada_w, ada_b, norm_mix_g, norm_mlp_g, w_o, mlp_w1, mlp_b1, mlp_w2, mlp_b2, w_qkv_a, rpb_a,
           w_qkv_b, sink_b, w_qkv_c, q_norm_c, k_norm_c, final_norm_g):
    batch, seq, _ = x_prompt.shape
    dec_batch, dec_seq, _ = x_sample.shape
    depth = ada_w.shape[0]
    assert dec_batch + 1 <= COND_ROWS and dec_seq % GRID_W == 0
    assert (dec_seq // GRID_W) % NA_ROWS == 0 and dec_seq // GRID_W >= 3 * NA_ROWS

    cond = jnp.concatenate([c, c_ctx[None, :], jnp.zeros((COND_ROWS - dec_batch - 1, D_MODEL), F32)], axis=0)
    mods = _ada_mods(cond, ada_w, ada_b).reshape(depth, COND_ROWS, ADA_CHUNKS, D_MODEL)

    w_qkv = (w_qkv_a, w_qkv_b, w_qkv_c)
    n_kv = (N_KV_A, N_KV_GQA, N_KV_GQA)
    caches_k = (cache_k_a, cache_k_b, cache_k_c)
    caches_v = (cache_v_a, cache_v_b, cache_v_c)
    new_kv = [None] * N_MIXERS
    rope = _rope_tables(dec_seq)
    nq = N_HEADS * HEAD_DIM

    xp = x_prompt.reshape(batch * seq, D_MODEL)
    xs = x_sample.reshape(dec_batch * dec_seq, D_MODEL)

    def layer_weights_f32(l):
        m, j = l % N_MIXERS, l // N_MIXERS
        nk = n_kv[m] * HEAD_DIM
        return [(w_qkv[m], j, 0, nq), (w_qkv[m], j, nq, nk), (w_qkv[m], j, nq + nk, nk), (w_o, l, 0, D_MODEL),
                (mlp_w1, l, 0, D_FF), (mlp_w2, l, 0, D_MODEL)]

    first = layer_weights_f32(0)
    weights = [w[la, :, c0:c0 + nc].astype(BF16) for w, la, c0, nc in first[:3]] + [None] * 3

    for l in range(depth):
        m, j = l % N_MIXERS, l // N_MIXERS
        wq, wk, wv = weights[:3]
        b1 = mlp_b1[l][None, :]
        b2 = mlp_b2[l][None, :]
        g_mix = norm_mix_g[l][None, :]
        g_mlp = norm_mlp_g[l][None, :]
        mods_ctx = mods[l, dec_batch:dec_batch + 1]
        mods_lat = mods[l, :dec_batch]
        final = final_norm_g[None, :] if l == depth - 1 else None
        norm_args, ctx_norm_args = {}, {}
        if m == 2:
            qn = jnp.tile(q_norm_c[j], LANES // HEAD_DIM)[None, :]
            norm_args = dict(qn=qn, kn=jnp.tile(k_norm_c[j], LANES // HEAD_DIM)[None, :])
            ctx_norm_args = dict(qn=qn, kn_col=jnp.tile(k_norm_c[j], n_kv[m])[:, None])
        sink = sink_b[j] if m == 1 else None

        o_ctx, kt, vt = _ctx_front(xp, mods_ctx, g_mix, wq, wk.T, wv.T, n_kv=n_kv[m], seq=seq, prev=new_kv[m],
                                   sink=sink, **ctx_norm_args)
        new_kv[m] = (kt, vt)

        if m == 0:
            q, k, v, casted = _qkv_proj(xs, mods_lat, g_mix, wq, wk, wv, n_kv=n_kv[m], dup_kv=False,
                                        to_cast=first[3:] if l == 0 else ())
            if l == 0:
                weights[3:] = casted
            o_lat = _na_attention(q, k, v, caches_k[m], caches_v[m], _na_padded_rpb(rpb_a[j]),
                                  batch=dec_batch, seq=dec_seq, layer_slot=j)
        else:
            q, k, v, _ = _qkv_proj(xs, mods_lat, g_mix, wq, wk, wv, n_kv=n_kv[m], dup_kv=True,
                                   rope=rope, seq=dec_seq, **norm_args)
            o_lat = _gqa_attention(q, k, v, caches_k[m], caches_v[m], batch=dec_batch, seq=dec_seq, layer_slot=j,
                                   sink=sink)

        wo, w1, w2 = weights[3:]
        to_cast = layer_weights_f32(l + 1) if l + 1 < depth else ()
        xp, xs, weights = _mlp_block(xp, o_ctx, xs, o_lat, mods[l, :dec_batch + 1], wo, g_mlp, w1, b1, w2, b2, final,
                                     to_cast=to_cast)

    y_prompt = xp.reshape(batch, seq, D_MODEL)
    y_sample = xs.reshape(dec_batch, dec_seq, D_MODEL)
    outs = [y_prompt, y_sample]
    for m in range(N_MIXERS):
        for stacked in new_kv[m]:
            t = stacked.reshape(batch, stacked.shape[1], n_kv[m], HEAD_DIM, seq)
            outs.append(jnp.transpose(t, (0, 1, 4, 2, 3)))
    return tuple(outs)
```
